```python
import jax
import jax.numpy as jnp
from jax import lax
import numpy as np

D_MODEL = 1024
BATCH = 2
SEQ = 8192
DEPTH = 2
DEC_BATCH = 128
DEC_SEQ = 4
PAST_LEN = 16384
PAGE_SIZE = 128

N_MIXERS = 2
N_A = (DEPTH + 1) // 2
N_B = DEPTH // 2
D_PLE = 256
EPS = 1e-6
D_RNN = D_MODEL
N_LRU_BLOCKS = 8
LRU_BLOCK = D_RNN // N_LRU_BLOCKS
CONV_W = 4
LRU_C = 8.0
HEAD_DIM = 64
N_HEADS = D_MODEL // HEAD_DIM
N_KV = N_HEADS // 4
GROUP = N_HEADS // N_KV
Q_W = N_HEADS * HEAD_DIM
KV_W = N_KV * HEAD_DIM
D_ATT = Q_W
WINDOW = 128
BLOCK = 128
ALIBI_MAX = 8.0
NEG_INF = -1e30

kernel_name = 'hybrid_rglru_swa_sink_decoder_step'


def _rmsnorm(x, g):
    xf = x.astype(jnp.float32)
    xf = xf * lax.rsqrt(jnp.mean(xf * xf, axis=-1, keepdims=True) + EPS)
    return (xf * g.astype(jnp.float32)).astype(x.dtype)


def _alibi_slopes():
    return jnp.exp2(-ALIBI_MAX * jnp.arange(1, N_HEADS + 1, dtype=jnp.float32) / N_HEADS)


def _causal_conv(x, prev, w, b):
    T = x.shape[1]
    xp = jnp.concatenate([prev.astype(x.dtype), x], axis=1)
    y = b
    for k in range(CONV_W):
        y = y + xp[:, k:k + T] * w[k]
    return y, xp[:, -(CONV_W - 1):]


def _block_diag(x, w, b):
    xb = x.reshape(x.shape[:-1] + (N_LRU_BLOCKS, LRU_BLOCK))
    return jnp.einsum('...nc,ncd->...nd', xb, w).reshape(x.shape) + b


def _rg_lru(xc, h0, wr, br, wi, bi, lam):
    xf = xc.astype(jnp.float32)
    r = jax.nn.sigmoid(_block_diag(xf, wr.astype(jnp.float32), br.astype(jnp.float32)))
    i = jax.nn.sigmoid(_block_diag(xf, wi.astype(jnp.float32), bi.astype(jnp.float32)))
    log_a = -LRU_C * r * jax.nn.softplus(-lam.astype(jnp.float32))
    a = jnp.exp(log_a)
    b = jnp.sqrt(-jnp.expm1(2.0 * log_a)) * (i * xf)
    b = b.at[:, 0].add(a[:, 0] * h0.astype(jnp.float32))

    def combine(left, right):
        return left[0] * right[0], right[0] * left[1] + right[1]

    _, h = lax.associative_scan(combine, (a, b), axis=1)
    return h, h[:, -1]


def _mixer_a(u, h0, conv_prev, w_in, cw, cb, wr, br, wi, bi, lam, w_out):
    xg = u @ w_in
    xb, gate = xg[..., :D_RNN], xg[..., D_RNN:]
    xc, conv_new = _causal_conv(xb, conv_prev, cw, cb)
    h, h_last = _rg_lru(xc, h0, wr, br, wi, bi, lam)
    y = (h.astype(u.dtype) * jax.nn.silu(gate)) @ w_out
    return y, h_last, conv_new


def _sink_attention(q, k, v, dist, valid, sinks, slopes):
    s = jnp.einsum('...qkgd,...skd->...kgqs', q.astype(jnp.float32), k.astype(jnp.float32)) * (HEAD_DIM ** -0.5)
    s = s - slopes[:, :, None, None] * dist.astype(jnp.float32)
    s = jnp.where(valid, s, NEG_INF)
    sink_col = jnp.broadcast_to(sinks.astype(jnp.float32)[:, :, None, None], s.shape[:-1] + (1,))
    p = jax.nn.softmax(jnp.concatenate([s, sink_col], axis=-1), axis=-1)[..., :-1]
    return jnp.einsum('...kgqs,...skd->...qkgd', p, v.astype(jnp.float32))


def _mixer_b(u, kbuf, vbuf, w_in, sinks, w_out):
    Bn, T, _ = u.shape
    proj = u @ w_in
    q = proj[..., :Q_W].reshape(Bn, T, N_KV, GROUP, HEAD_DIM)
    k = proj[..., Q_W:Q_W + KV_W].reshape(Bn, T, N_KV, HEAD_DIM)
    v = proj[..., Q_W + KV_W:Q_W + 2 * KV_W].reshape(Bn, T, N_KV, HEAD_DIM)
    gate = proj[..., Q_W + 2 * KV_W:]
    sink = sinks.reshape(N_KV, GROUP)
    slopes = _alibi_slopes().reshape(N_KV, GROUP)
    if kbuf is None:
        nb = T // BLOCK
        qb = q.reshape(Bn, nb, BLOCK, N_KV, GROUP, HEAD_DIM)
        kb = k.reshape(Bn, nb, BLOCK, N_KV, HEAD_DIM)
        vb = v.reshape(Bn, nb, BLOCK, N_KV, HEAD_DIM)

        def band(xb):
            prev = jnp.concatenate([jnp.zeros_like(xb[:, :1]), xb[:, :-1]], axis=1)
            return jnp.concatenate([prev, xb], axis=2)

        qi = jnp.arange(BLOCK)[:, None]
        si = jnp.arange(2 * BLOCK)[None, :]
        dist = BLOCK + qi - si
        blk = jnp.arange(nb)[:, None, None]
        valid = (dist >= 0) & (dist < WINDOW) & ((blk - 1) * BLOCK + si >= 0)
        o = _sink_attention(qb, band(kb), band(vb), dist, valid[:, None, None], sink, slopes)
        new_k, new_v = k[:, -WINDOW:], v[:, -WINDOW:]
    else:
        W = kbuf.shape[1]
        k_all = jnp.concatenate([kbuf.astype(k.dtype), k], axis=1)
        v_all = jnp.concatenate([vbuf.astype(v.dtype), v], axis=1)
        qi = jnp.arange(T)[:, None]
        si = jnp.arange(W + T)[None, :]
        dist = W + qi - si
        valid = (dist >= 0) & (dist < WINDOW)
        o = _sink_attention(q, k_all, v_all, dist, valid, sink, slopes)
        new_k, new_v = k_all[:, -WINDOW:], v_all[:, -WINDOW:]
    o = o.reshape(Bn, T, Q_W).astype(u.dtype)
    return (o * jax.nn.silu(gate)) @ w_out, new_k, new_v


def _trunk(x, p, lru_h, conv_st, kc, vc, weights):
    (norm_g, final_norm_g, ple_norm_g, w_ple_gate, w_ple_proj, w_in_a, conv_w_a, conv_b_a,
     w_rgate, b_rgate, w_igate, b_igate, lru_lambda, w_out_a, w_in_b, sinks, w_out_b) = weights
    Bn = x.shape[0]
    new_h, new_conv, new_k, new_v = [], [], [], []
    h = x
    for i in range(DEPTH):
        j = i // N_MIXERS
        u = _rmsnorm(h, norm_g[i])
        if i % N_MIXERS == 0:
            h0 = jnp.zeros((Bn, D_RNN), jnp.float32) if lru_h is None else lru_h[j]
            cprev = jnp.zeros((Bn, CONV_W - 1, D_RNN), x.dtype) if conv_st is None else conv_st[j]
            y, hl, cn = _mixer_a(u, h0, cprev, w_in_a[j], conv_w_a[j], conv_b_a[j], w_rgate[j], b_rgate[j],
                                 w_igate[j], b_igate[j], lru_lambda[j], w_out_a[j])
            new_h.append(hl)
            new_conv.append(cn)
        else:
            y, nk, nv = _mixer_b(u, None if kc is None else kc[j], None if vc is None else vc[j],
                                 w_in_b[j], sinks[j], w_out_b[j])
            new_k.append(nk)
            new_v.append(nv)
        h = h + y
        gate = jax.nn.sigmoid((_rmsnorm(h, ple_norm_g[i]) @ w_ple_gate[i]).astype(jnp.float32)).astype(h.dtype)
        h = h + gate * (p[i] @ w_ple_proj[i])
    return (_rmsnorm(h, final_norm_g), jnp.stack(new_h), jnp.stack(new_conv), jnp.stack(new_k), jnp.stack(new_v))


def setup_inputs(seed: int = 0) -> dict:
    key = jax.random.key(seed)
    ks = jax.random.split(key, 26)

    def nrm(k, shape, scale):
        return jax.random.normal(k, shape, jnp.float32) * scale

    a0 = jax.random.uniform(ks[20], (N_A, D_RNN), jnp.float32, 0.9, 0.999)
    s0 = a0 ** (1.0 / LRU_C)
    return {
        'x_prompt': nrm(ks[0], (BATCH, SEQ, D_MODEL), 1.0),
        'x_sample': nrm(ks[1], (DEC_BATCH, DEC_SEQ, D_MODEL), 1.0),
        'p_prompt': nrm(ks[2], (DEPTH, BATCH, SEQ, D_PLE), 1.0),
        'p_sample': nrm(ks[3], (DEPTH, DEC_BATCH, DEC_SEQ, D_PLE), 1.0),
        'state_lru_h': nrm(ks[4], (N_A, DEC_BATCH, D_RNN), 0.5),
        'state_conv': nrm(ks[5], (N_A, DEC_BATCH, CONV_W - 1, D_RNN), 1.0),
        'cache_k': nrm(ks[6], (N_B, DEC_BATCH, WINDOW, N_KV, HEAD_DIM), 1.0),
        'cache_v': nrm(ks[7], (N_B, DEC_BATCH, WINDOW, N_KV, HEAD_DIM), 1.0),
        'norm_g': 1.0 + nrm(ks[8], (DEPTH, D_MODEL), 0.02),
        'final_norm_g': 1.0 + nrm(ks[9], (D_MODEL,), 0.02),
        'ple_norm_g': 1.0 + nrm(ks[10], (DEPTH, D_MODEL), 0.02),
        'w_ple_gate': nrm(ks[11], (DEPTH, D_MODEL, D_MODEL), D_MODEL ** -0.5),
        'w_ple_proj': nrm(ks[12], (DEPTH, D_PLE, D_MODEL), D_PLE ** -0.5),
        'w_in_a': nrm(ks[13], (N_A, D_MODEL, 2 * D_RNN), D_MODEL ** -0.5),
        'conv_w_a': nrm(ks[14], (N_A, CONV_W, D_RNN), CONV_W ** -0.5),
        'conv_b_a': nrm(ks[15], (N_A, D_RNN), 0.01),
        'w_rgate': nrm(ks[16], (N_A, N_LRU_BLOCKS, LRU_BLOCK, LRU_BLOCK), LRU_BLOCK ** -0.5),
        'b_rgate': nrm(ks[17], (N_A, D_RNN), 0.01),
        'w_igate': nrm(ks[18], (N_A, N_LRU_BLOCKS, LRU_BLOCK, LRU_BLOCK), LRU_BLOCK ** -0.5),
        'b_igate': nrm(ks[19], (N_A, D_RNN), 0.01),
        'lru_lambda': jnp.log(s0) - jnp.log1p(-s0),
        'w_out_a': nrm(ks[21], (N_A, D_RNN, D_MODEL), D_RNN ** -0.5),
        'w_in_b': nrm(ks[22], (N_B, D_MODEL, Q_W + 2 * KV_W + D_ATT), D_MODEL ** -0.5),
        'sinks': nrm(ks[23], (N_B, N_HEADS), 0.5),
        'w_out_b': nrm(ks[24], (N_B, D_ATT, D_MODEL), D_ATT ** -0.5),
    }


def reference(x_prompt, x_sample, p_prompt, p_sample, state_lru_h, state_conv, cache_k, cache_v,
              norm_g, final_norm_g, ple_norm_g, w_ple_gate, w_ple_proj, w_in_a, conv_w_a, conv_b_a,
              w_rgate, b_rgate, w_igate, b_igate, lru_lambda, w_out_a, w_in_b, sinks, w_out_b):
    weights = (norm_g, final_norm_g, ple_norm_g, w_ple_gate, w_ple_proj, w_in_a, conv_w_a, conv_b_a,
               w_rgate, b_rgate, w_igate, b_igate, lru_lambda, w_out_a, w_in_b, sinks, w_out_b)
    y_prompt, h_p, conv_p, k_p, v_p = _trunk(x_prompt, p_prompt, None, None, None, None, weights)
    y_sample, h_s, conv_s, k_s, v_s = _trunk(x_sample, p_sample, state_lru_h, state_conv, cache_k, cache_v, weights)
    return (y_prompt, y_sample, h_p, conv_p, k_p, v_p, h_s, conv_s, k_s, v_s)
```

```python
import functools

import jax
import jax.numpy as jnp
from jax import lax
from jax.experimental import pallas as pl
from jax.experimental.pallas import tpu as pltpu

D_MODEL = 1024
D_RNN = 1024
D_PLE = 256
N_LRU_BLOCKS = 8
LRU_BLOCK = 128
CONV_W = 4
LRU_C = 8.0
EPS = 1e-6
HEAD_DIM = 64
N_HEADS = 16
N_KV = 4
GROUP = 4
Q_W = 1024
KV_W = 256
WINDOW = 128
BLOCK = 128
NEG_INF = -1e30
QK_SCALE = HEAD_DIM ** -0.5

LANES = 128
SUBLANES = 8
VMEM_LIMIT = 56 * 1024 * 1024

PROMPT_TILE = 256
SAMPLE_SEQS = 8

F32 = jnp.float32
BF16 = jnp.bfloat16


def _rms(x, g):
    ms = jnp.mean(x * x, axis=-1, keepdims=True)
    return x * lax.rsqrt(ms + EPS) * g


def _sigmoid(x):
    return 0.5 * jnp.tanh(0.5 * x) + 0.5


def _silu(x):
    return x * _sigmoid(x)


def _dot(a, w):
    return jnp.dot(a.astype(BF16), w, preferred_element_type=F32)


def _softplus(z):
    return jnp.maximum(z, 0.0) + jnp.log1p(jnp.exp(-jnp.abs(z)))


def _lru_coeffs(xc, wg_ref, b_r, b_i, lam):
    xcb = xc.astype(BF16)
    zr, zi = [], []
    for n in range(N_LRU_BLOCKS):
        z = jnp.dot(xcb[:, n * LRU_BLOCK:(n + 1) * LRU_BLOCK], wg_ref[n], preferred_element_type=F32)
        zr.append(z[:, :LRU_BLOCK])
        zi.append(z[:, LRU_BLOCK:])
    r = _sigmoid(jnp.concatenate(zr, axis=1) + b_r)
    i = _sigmoid(jnp.concatenate(zi, axis=1) + b_i)
    log_a = (-LRU_C * _softplus(-lam)) * r
    a = jnp.exp(log_a)
    b = jnp.sqrt(-jnp.tanh(log_a) * (a * a + 1.0)) * (i * xc)
    return a, b


def _ple(h, p, g, wpg_ref, wpp_ref):
    gate = _sigmoid(_dot(_rms(h, g), wpg_ref[...]))
    return h + gate * _dot(p, wpp_ref[...])


def _scan_rows(a, b, h0):
    t_rows, width = a.shape
    row = lax.broadcasted_iota(jnp.int32, (SUBLANES, width), 0)
    masks = [(s, row >= s) for s in (1, 2, 4)]
    carry = jnp.broadcast_to(h0, (SUBLANES, width))
    out = []
    for g in range(t_rows // SUBLANES):
        ag = a[g * SUBLANES:(g + 1) * SUBLANES]
        bg = b[g * SUBLANES:(g + 1) * SUBLANES]
        for s, m in masks:
            a_prev = pltpu.roll(ag, s, 0)
            b_prev = pltpu.roll(bg, s, 0)
            bg = bg + jnp.where(m, ag, 0.0) * b_prev
            ag = jnp.where(m, ag * a_prev, ag)
        hg = bg + ag * carry
        carry = jnp.broadcast_to(hg[SUBLANES - 1:SUBLANES], (SUBLANES, width))
        out.append(hg)
    return jnp.concatenate(out, axis=0)


def _shift_rows(x, prev8, k):
    row = lax.broadcasted_iota(jnp.int32, (SUBLANES, x.shape[1]), 0)
    first = row < k
    pieces = []
    prev = pltpu.roll(prev8, k, 0)
    for g in range(x.shape[0] // SUBLANES):
        cur = pltpu.roll(x[g * SUBLANES:(g + 1) * SUBLANES], k, 0)
        pieces.append(jnp.where(first, prev, cur))
        prev = cur
    return jnp.concatenate(pieces, axis=0)


V0_NORM, V0_PLE, V0_CB, V0_BR, V0_BI, V0_LAM, V0_CW = 0, 1, 2, 3, 4, 5, 6
V1_NORM, V1_PLE, V1_FINAL = 0, 1, 2


def _prompt_l0_kernel(x_ref, p_ref, vec_ref, win_ref, wg_ref, wout_ref, wpg_ref, wpp_ref,
                      h1_ref, hl_ref, tail_ref, prev8, hcar):
    t_rows = x_ref.shape[0]

    @pl.when(pl.program_id(1) == 0)
    def _():
        prev8[...] = jnp.zeros_like(prev8)
        hcar[...] = jnp.zeros_like(hcar)

    x = x_ref[...]
    xg = _dot(_rms(x, vec_ref[V0_NORM:V0_NORM + 1]), win_ref[...])
    xb = xg[:, :D_RNN]
    gate = xg[:, D_RNN:]
    p8 = prev8[...]
    xc = vec_ref[V0_CB:V0_CB + 1] + vec_ref[V0_CW + 3:V0_CW + 4] * xb
    for k in range(1, CONV_W):
        xc = xc + vec_ref[V0_CW + 3 - k:V0_CW + 4 - k] * _shift_rows(xb, p8, k)
    prev8[...] = xb[t_rows - SUBLANES:]
    a, b = _lru_coeffs(xc, wg_ref, vec_ref[V0_BR:V0_BR + 1], vec_ref[V0_BI:V0_BI + 1],
                       vec_ref[V0_LAM:V0_LAM + 1])
    h = _scan_rows(a, b, hcar[...])
    hcar[...] = h[t_rows - 1:]
    h1 = x + _dot(h * _silu(gate), wout_ref[...])
    h1_ref[...] = _ple(h1, p_ref[...], vec_ref[V0_PLE:V0_PLE + 1], wpg_ref, wpp_ref)
    hl_ref[...] = h[t_rows - SUBLANES:]
    tail_ref[...] = xb[t_rows - SUBLANES:]


def _const_spec(shape):
    return pl.BlockSpec(shape, lambda *_: (0,) * len(shape), pipeline_mode=pl.Buffered(1))


def _prompt_l0(x, p0, vec0, w_in_a, wg, w_out_a, wpg0, wpp0):
    nb, seq, _ = x.shape
    t = PROMPT_TILE
    return pl.pallas_call(
        _prompt_l0_kernel,
        grid=(nb, seq // t),
        in_specs=[
            pl.BlockSpec((None, t, D_MODEL), lambda b, s: (b, s, 0)),
            pl.BlockSpec((None, t, D_PLE), lambda b, s: (b, s, 0)),
            _const_spec(vec0.shape), _const_spec(w_in_a.shape), _const_spec(wg.shape),
            _const_spec(w_out_a.shape), _const_spec(wpg0.shape), _const_spec(wpp0.shape),
        ],
        out_specs=[
            pl.BlockSpec((None, t, D_MODEL), lambda b, s: (b, s, 0)),
            pl.BlockSpec((None, SUBLANES, D_RNN), lambda b, s: (b, 0, 0)),
            pl.BlockSpec((None, SUBLANES, D_RNN), lambda b, s: (b, 0, 0)),
        ],
        out_shape=[
            jax.ShapeDtypeStruct((nb, seq, D_MODEL), F32),
            jax.ShapeDtypeStruct((nb, SUBLANES, D_RNN), F32),
            jax.ShapeDtypeStruct((nb, SUBLANES, D_RNN), F32),
        ],
        scratch_shapes=[pltpu.VMEM((SUBLANES, D_RNN), F32), pltpu.VMEM((1, D_RNN), F32)],
        compiler_params=pltpu.CompilerParams(
            dimension_semantics=("arbitrary", "arbitrary"), vmem_limit_bytes=VMEM_LIMIT),
        name="prompt_l0",
    )(x, p0, vec0, w_in_a, wg, w_out_a, wpg0, wpp0)


def _slope(head):
    return 2.0 ** (-8.0 * (head + 1) / N_HEADS)


def _build_prompt_bias(bias_ref):
    shape = (BLOCK, 4 * BLOCK)
    qi = lax.broadcasted_iota(jnp.int32, shape, 0)
    col = lax.broadcasted_iota(jnp.int32, shape, 1)
    si = col & (2 * BLOCK - 1)
    second = col >= 2 * BLOCK
    dist = BLOCK + qi - si
    valid = (dist >= 0) & (dist < WINDOW)
    valid_first = valid & (si >= BLOCK)
    distf = dist.astype(F32)
    for hh in range(GROUP):
        for i in range(N_KV // 2):
            slope = jnp.where(second, _slope(GROUP * (2 * i + 1) + hh), _slope(GROUP * (2 * i) + hh))
            b = -(slope * distf)
            bias_ref[0, hh * 2 + i] = jnp.where(valid_first, b, NEG_INF)
            bias_ref[1, hh * 2 + i] = jnp.where(valid, b, NEG_INF)


def _prompt_l1_kernel(sink_ref, h1_ref, p_ref, vec_ref, win_ref, wout_ref, wpg_ref, wpp_ref,
                      y_ref, kl_ref, vl_ref, kprev, vprev, bias_ref):
    t_rows = h1_ref.shape[0]
    s_idx = pl.program_id(1)

    @pl.when(jnp.logical_and(pl.program_id(0) == 0, s_idx == 0))
    def _():
        _build_prompt_bias(bias_ref)

    @pl.when(s_idx == 0)
    def _():
        kprev[...] = jnp.zeros_like(kprev)
        vprev[...] = jnp.zeros_like(vprev)

    h1 = h1_ref[...]
    proj = _dot(_rms(h1, vec_ref[V1_NORM:V1_NORM + 1]), win_ref[...])
    q = (proj[:, :Q_W] * QK_SCALE).astype(BF16)
    k = proj[:, Q_W:Q_W + KV_W]
    v = proj[:, Q_W + KV_W:Q_W + 2 * KV_W]
    gate = proj[:, Q_W + 2 * KV_W:]
    lo = lax.broadcasted_iota(jnp.int32, (1, LANES), 1) < HEAD_DIM
    first_tbl = jnp.where(s_idx == 0, 0, 1)

    blocks = []
    for j in range(t_rows // BLOCK):
        rows = slice(j * BLOCK, (j + 1) * BLOCK)
        k_prev = kprev[...] if j == 0 else k[(j - 1) * BLOCK:j * BLOCK]
        v_prev = vprev[...] if j == 0 else v[(j - 1) * BLOCK:j * BLOCK]
        tbl = first_tbl if j == 0 else 1
        cols = [None] * (2 * GROUP)
        for i in range(N_KV // 2):
            lanes = slice(i * LANES, (i + 1) * LANES)
            k2 = jnp.concatenate([k_prev[:, lanes], k[rows, lanes]], axis=0)
            v2 = jnp.concatenate([v_prev[:, lanes], v[rows, lanes]], axis=0)
            k_both = jnp.concatenate([jnp.where(lo, k2, 0.0), jnp.where(lo, 0.0, k2)], axis=0).astype(BF16)
            v_both = jnp.concatenate([jnp.where(lo, v2, 0.0), jnp.where(lo, 0.0, v2)], axis=0).astype(BF16)
            for hh in range(GROUP):
                q2 = q[rows, hh * KV_W + i * LANES:hh * KV_W + (i + 1) * LANES]
                s2 = lax.dot_general(q2, k_both, (((1,), (1,)), ((), ())), preferred_element_type=F32)
                s2 = s2 + bias_ref[tbl, hh * 2 + i]
                es, inv = [], []
                for half in range(2):
                    sink = sink_ref[0, hh * N_KV + 2 * i + half]
                    sh = s2[:, half * 2 * BLOCK:(half + 1) * 2 * BLOCK]
                    m = jnp.maximum(jnp.max(sh, axis=-1, keepdims=True), sink)
                    e = jnp.exp(sh - m)
                    denom = jnp.sum(e, axis=-1, keepdims=True) + jnp.exp(sink - m)
                    es.append(e.astype(BF16))
                    inv.append(1.0 / denom)
                o2 = jnp.dot(jnp.concatenate(es, axis=1), v_both, preferred_element_type=F32)
                cols[hh * 2 + i] = o2 * jnp.where(lo, inv[0], inv[1])
        blocks.append(jnp.concatenate(cols, axis=1))
    o = jnp.concatenate(blocks, axis=0)

    k_last = k[t_rows - BLOCK:]
    v_last = v[t_rows - BLOCK:]
    kprev[...] = k_last
    vprev[...] = v_last
    kl_ref[...] = k_last
    vl_ref[...] = v_last
    h2 = h1 + _dot(o * _silu(gate), wout_ref[...])
    h2 = _ple(h2, p_ref[...], vec_ref[V1_PLE:V1_PLE + 1], wpg_ref, wpp_ref)
    y_ref[...] = _rms(h2, vec_ref[V1_FINAL:V1_FINAL + 1])


def _prompt_l1(sinks_p, h1, p1, vec1, w_in_b, w_out_b, wpg1, wpp1):
    nb, seq, _ = h1.shape
    t = PROMPT_TILE
    return pl.pallas_call(
        _prompt_l1_kernel,
        grid=(nb, seq // t),
        in_specs=[
            pl.BlockSpec(memory_space=pltpu.SMEM),
            pl.BlockSpec((None, t, D_MODEL), lambda b, s: (b, s, 0)),
            pl.BlockSpec((None, t, D_PLE), lambda b, s: (b, s, 0)),
            _const_spec(vec1.shape), _const_spec(w_in_b.shape), _const_spec(w_out_b.shape),
            _const_spec(wpg1.shape), _const_spec(wpp1.shape),
        ],
        out_specs=[
            pl.BlockSpec((None, t, D_MODEL), lambda b, s: (b, s, 0)),
            pl.BlockSpec((None, BLOCK, KV_W), lambda b, s: (b, 0, 0)),
            pl.BlockSpec((None, BLOCK, KV_W), lambda b, s: (b, 0, 0)),
        ],
        out_shape=[
            jax.ShapeDtypeStruct((nb, seq, D_MODEL), F32),
            jax.ShapeDtypeStruct((nb, BLOCK, KV_W), F32),
            jax.ShapeDtypeStruct((nb, BLOCK, KV_W), F32),
        ],
        scratch_shapes=[
            pltpu.VMEM((BLOCK, KV_W), F32), pltpu.VMEM((BLOCK, KV_W), F32),
            pltpu.VMEM((2, 2 * GROUP, BLOCK, 4 * BLOCK), F32),
        ],
        compiler_params=pltpu.CompilerParams(
            dimension_semantics=("arbitrary", "arbitrary"), vmem_limit_bytes=VMEM_LIMIT),
        name="prompt_l1",
    )(sinks_p, h1, p1, vec1, w_in_b, w_out_b, wpg1, wpp1)


def _sample_l0_kernel(x_ref, p_ref, h0_ref, cst_ref, vec_ref, vec1_ref, win_ref, wg_ref, wout_ref, wpg_ref, wpp_ref,
                      winb_ref, h1_ref, hl_ref, cnew_ref, q_ref, k_ref, v_ref, gate_ref):
    nseq = h0_ref.shape[0]
    steps = x_ref.shape[0] // nseq
    x = x_ref[...]
    xg = _dot(_rms(x, vec_ref[V0_NORM:V0_NORM + 1]), win_ref[...])
    xb = xg[:, :D_RNN]
    gate = xg[:, D_RNN:]
    hist = [cst_ref[k] for k in range(CONV_W - 1)] + [xb[t * nseq:(t + 1) * nseq] for t in range(steps)]
    xc = []
    for t in range(steps):
        acc = vec_ref[V0_CB:V0_CB + 1] + vec_ref[V0_CW:V0_CW + 1] * hist[t]
        for k in range(1, CONV_W):
            acc = acc + vec_ref[V0_CW + k:V0_CW + k + 1] * hist[t + k]
        xc.append(acc)
    for k in range(CONV_W - 1):
        cnew_ref[k] = hist[steps + k]
    a, b = _lru_coeffs(jnp.concatenate(xc, axis=0), wg_ref, vec_ref[V0_BR:V0_BR + 1], vec_ref[V0_BI:V0_BI + 1],
                       vec_ref[V0_LAM:V0_LAM + 1])
    h = h0_ref[...]
    hs = []
    for t in range(steps):
        h = a[t * nseq:(t + 1) * nseq] * h + b[t * nseq:(t + 1) * nseq]
        hs.append(h)
    hl_ref[...] = h
    h1 = x + _dot(jnp.concatenate(hs, axis=0) * _silu(gate), wout_ref[...])
    h1 = _ple(h1, p_ref[...], vec_ref[V0_PLE:V0_PLE + 1], wpg_ref, wpp_ref)
    h1_ref[...] = h1
    proj = _dot(_rms(h1, vec1_ref[V1_NORM:V1_NORM + 1]), winb_ref[...])
    q_ref[...] = proj[:, :Q_W] * QK_SCALE
    k_ref[...] = proj[:, Q_W:Q_W + KV_W]
    v_ref[...] = proj[:, Q_W + KV_W:Q_W + 2 * KV_W]
    gate_ref[...] = proj[:, Q_W + 2 * KV_W:]


def _sample_l0(x_tm, p0_tm, h0, cst_tm, vec0, vec1, w_in_a, wg, w_out_a, wpg0, wpp0, w_in_b):
    rows = x_tm.shape[0]
    nseq = h0.shape[0]
    args = (x_tm, p0_tm, h0, cst_tm, vec0, vec1, w_in_a, wg, w_out_a, wpg0, wpp0, w_in_b)
    out_shapes = [(rows, D_MODEL), (nseq, D_RNN), (CONV_W - 1, nseq, D_RNN),
                  (rows, Q_W), (rows, KV_W), (rows, KV_W), (rows, Q_W)]
    return pl.pallas_call(
        _sample_l0_kernel,
        grid=(1,),
        in_specs=[_const_spec(a.shape) for a in args],
        out_specs=[pl.BlockSpec(s, lambda *_, n=len(s): (0,) * n) for s in out_shapes],
        out_shape=[jax.ShapeDtypeStruct(s, F32) for s in out_shapes],
        compiler_params=pltpu.CompilerParams(dimension_semantics=("arbitrary",), vmem_limit_bytes=VMEM_LIMIT),
        name="sample_l0",
    )(*args)


def _sample_attn_kernel(q_ref, knt_ref, vnt_ref, kc_ref, vc_ref, rowinfo_ref, o_ref, nk_ref, nv_ref):
    nseq = q_ref.shape[0]
    nq = q_ref.shape[1]
    steps = nq // GROUP
    rows = N_KV * nq
    sink = rowinfo_ref[:, 0:1]
    slope = rowinfo_ref[:, 1:2]
    tq = rowinfo_ref[:, 2:3]
    col = lax.broadcasted_iota(jnp.int32, (rows, 2 * WINDOW), 1).astype(F32)
    dist = WINDOW + tq - col
    valid = (dist >= 0.0) & (dist < WINDOW)
    bias = jnp.where(valid, -(slope * dist), NEG_INF)
    lane_grp = lax.broadcasted_iota(jnp.int32, (1, KV_W), 1) // HEAD_DIM
    new_lane = lax.broadcasted_iota(jnp.int32, (1, WINDOW), 1) < steps
    knt_all = knt_ref[...]
    vnt_all = vnt_ref[...]

    for bl in range(nseq):
        q = q_ref[bl]
        q_full = jnp.concatenate([jnp.where(lane_grp == g, q, 0.0) for g in range(N_KV)], axis=0).astype(BF16)
        shift = (LANES - steps * bl) % LANES
        knt = knt_all if shift == 0 else pltpu.roll(knt_all, shift, 1)
        vnt = vnt_all if shift == 0 else pltpu.roll(vnt_all, shift, 1)
        kt_ext = jnp.concatenate([kc_ref[bl], jnp.where(new_lane, knt, 0.0)], axis=1)
        vt_ext = jnp.concatenate([vc_ref[bl], jnp.where(new_lane, vnt, 0.0)], axis=1)
        s = jnp.dot(q_full, kt_ext.astype(BF16), preferred_element_type=F32) + bias
        m = jnp.maximum(jnp.max(s, axis=-1, keepdims=True), sink)
        e = jnp.exp(s - m)
        denom = jnp.sum(e, axis=-1, keepdims=True) + jnp.exp(sink - m)
        o = lax.dot_general(e.astype(BF16), vt_ext.astype(BF16), (((1,), (1,)), ((), ())),
                            preferred_element_type=F32) * (1.0 / denom)
        acc = jnp.where(lane_grp == 0, o[:nq], 0.0)
        for g in range(1, N_KV):
            acc = acc + jnp.where(lane_grp == g, o[g * nq:(g + 1) * nq], 0.0)
        o_ref[bl] = acc
        nk_ref[bl] = pltpu.roll(kt_ext, 2 * WINDOW - steps, 1)[:, :WINDOW]
        nv_ref[bl] = pltpu.roll(vt_ext, 2 * WINDOW - steps, 1)[:, :WINDOW]


def _sample_attn(q16, knt, vnt, kc, vc, rowinfo):
    nseq_total, nq, _ = q16.shape
    sb = SAMPLE_SEQS
    blk3 = lambda i: (i, 0, 0)
    return pl.pallas_call(
        _sample_attn_kernel,
        grid=(nseq_total // sb,),
        in_specs=[
            pl.BlockSpec((sb, nq, KV_W), blk3),
            pl.BlockSpec((None, KV_W, LANES), blk3),
            pl.BlockSpec((None, KV_W, LANES), blk3),
            pl.BlockSpec((sb, KV_W, WINDOW), blk3),
            pl.BlockSpec((sb, KV_W, WINDOW), blk3),
            pl.BlockSpec(rowinfo.shape, lambda i: (0, 0)),
        ],
        out_specs=[
            pl.BlockSpec((sb, nq, KV_W), blk3),
            pl.BlockSpec((sb, KV_W, WINDOW), blk3),
            pl.BlockSpec((sb, KV_W, WINDOW), blk3),
        ],
        out_shape=[
            jax.ShapeDtypeStruct((nseq_total, nq, KV_W), F32),
            jax.ShapeDtypeStruct((nseq_total, KV_W, WINDOW), F32),
            jax.ShapeDtypeStruct((nseq_total, KV_W, WINDOW), F32),
        ],
        compiler_params=pltpu.CompilerParams(dimension_semantics=("arbitrary",), vmem_limit_bytes=VMEM_LIMIT),
        name="sample_attn",
    )(q16, knt, vnt, kc, vc, rowinfo)


def _sample_tail_kernel(o_ref, gate_ref, h1_ref, p_ref, vec_ref, wout_ref, wpg_ref, wpp_ref, y_ref):
    o = jnp.concatenate([o_ref[hh] for hh in range(GROUP)], axis=1)
    h1 = h1_ref[...]
    h2 = h1 + _dot(o * _silu(gate_ref[...]), wout_ref[...])
    h2 = _ple(h2, p_ref[...], vec_ref[V1_PLE:V1_PLE + 1], wpg_ref, wpp_ref)
    y_ref[...] = _rms(h2, vec_ref[V1_FINAL:V1_FINAL + 1])


def _sample_tail(o4, gate, h1, p1_tm, vec1, w_out_b, wpg1, wpp1):
    args = (o4, gate, h1, p1_tm, vec1, w_out_b, wpg1, wpp1)
    return pl.pallas_call(
        _sample_tail_kernel,
        grid=(1,),
        in_specs=[_const_spec(a.shape) for a in args],
        out_specs=pl.BlockSpec(h1.shape, lambda i: (0, 0)),
        out_shape=jax.ShapeDtypeStruct(h1.shape, F32),
        compiler_params=pltpu.CompilerParams(dimension_semantics=("arbitrary",), vmem_limit_bytes=VMEM_LIMIT),
        name="sample_tail",
    )(*args)


def _heads_to_slot_major(w, axis):
    shape = w.shape
    w = w.reshape(shape[:axis] + (N_KV, GROUP, HEAD_DIM) + shape[axis + 1:])
    w = jnp.swapaxes(w, axis, axis + 1)
    return w.reshape(shape)


def _pad_rows(v, rows):
    return jnp.concatenate([v, jnp.zeros((rows - v.shape[0], v.shape[1]), v.dtype)], axis=0)


def kernel(x_prompt, x_sample, p_prompt, p_sample, state_lru_h, state_conv, cache_k, cache_v, norm_g, final_norm_g, ple_norm_g, w_ple_gate, w_ple_proj, w_in_a, conv_w_a, conv_b_a, w_rgate, b_rgate, w_igate, b_igate, lru_lambda, w_out_a, w_in_b, sinks, w_out_b):
    assert norm_g.shape[0] == 2 and w_in_a.shape[0] == 1 and w_in_b.shape[0] == 1
    nseq, steps, _ = x_sample.shape

    vec0 = _pad_rows(jnp.concatenate([norm_g[0:1], ple_norm_g[0:1], conv_b_a, b_rgate, b_igate, lru_lambda,
                                      conv_w_a[0]], axis=0), 2 * SUBLANES)
    vec1 = _pad_rows(jnp.concatenate([norm_g[1:2], ple_norm_g[1:2], final_norm_g[None]], axis=0), SUBLANES)
    w_in_a_b = w_in_a[0].astype(BF16)
    w_out_a_b = w_out_a[0].astype(BF16)
    wg = jnp.concatenate([w_rgate[0], w_igate[0]], axis=-1).astype(BF16)
    wpg = w_ple_gate.astype(BF16)
    wpp = w_ple_proj.astype(BF16)
    wb = w_in_b[0]
    w_in_b_b = jnp.concatenate([_heads_to_slot_major(wb[:, :Q_W], 1), wb[:, Q_W:Q_W + 2 * KV_W],
                                _heads_to_slot_major(wb[:, Q_W + 2 * KV_W:], 1)], axis=1).astype(BF16)
    w_out_b_b = _heads_to_slot_major(w_out_b[0], 0).astype(BF16)
    sinks_p = sinks[0].reshape(N_KV, GROUP).T.reshape(1, N_HEADS)

    h1_p, hl_p, tail_p = _prompt_l0(x_prompt, p_prompt[0], vec0, w_in_a_b, wg, w_out_a_b, wpg[0], wpp[0])
    y_prompt, kl_p, vl_p = _prompt_l1(sinks_p, h1_p, p_prompt[1], vec1, w_in_b_b, w_out_b_b, wpg[1], wpp[1])
    nbp = x_prompt.shape[0]
    new_h_p = hl_p[:, SUBLANES - 1][None]
    new_conv_p = tail_p[:, SUBLANES - (CONV_W - 1):][None]
    new_k_p = kl_p.reshape(1, nbp, WINDOW, N_KV, HEAD_DIM)
    new_v_p = vl_p.reshape(1, nbp, WINDOW, N_KV, HEAD_DIM)

    rows = steps * nseq
    x_tm = jnp.swapaxes(x_sample, 0, 1).reshape(rows, D_MODEL)
    p_tm = jnp.swapaxes(p_sample, 1, 2).reshape(2, rows, D_PLE)
    cst_tm = jnp.swapaxes(state_conv[0], 0, 1)
    h1_s, hl_s, cnew_tm, q_s, k_s, v_s, gate_s = _sample_l0(
        x_tm, p_tm[0], state_lru_h[0], cst_tm, vec0, vec1, w_in_a_b, wg, w_out_a_b, wpg[0], wpp[0], w_in_b_b)

    sb = SAMPLE_SEQS
    q16 = q_s.reshape(steps, nseq, GROUP, KV_W).transpose(1, 2, 0, 3).reshape(nseq, GROUP * steps, KV_W)

    def new_rows_t(a):
        a = a.reshape(steps, nseq // sb, sb, KV_W).transpose(1, 3, 2, 0).reshape(nseq // sb, KV_W, sb * steps)
        return jnp.concatenate([a, jnp.zeros((nseq // sb, KV_W, LANES - sb * steps), a.dtype)], axis=-1)

    kc = cache_k[0].transpose(0, 2, 3, 1).reshape(nseq, KV_W, WINDOW)
    vc = cache_v[0].transpose(0, 2, 3, 1).reshape(nseq, KV_W, WINDOW)
    head = jnp.arange(N_KV * GROUP * steps) // steps
    slopes = jnp.exp2(-8.0 * (head + 1).astype(F32) / N_HEADS)
    tq = (jnp.arange(N_KV * GROUP * steps) % steps).astype(F32)
    rowinfo = jnp.stack([sinks[0][head], slopes, tq], axis=1)
    rowinfo = jnp.concatenate([rowinfo, jnp.zeros((rowinfo.shape[0], LANES - 3), F32)], axis=1)
    o16, nk, nv = _sample_attn(q16, new_rows_t(k_s), new_rows_t(v_s), kc, vc, rowinfo)

    o4 = o16.reshape(nseq, GROUP, steps, KV_W).transpose(1, 2, 0, 3).reshape(GROUP, rows, KV_W)
    y_tm = _sample_tail(o4, gate_s, h1_s, p_tm[1], vec1, w_out_b_b, wpg[1], wpp[1])
    y_sample = jnp.swapaxes(y_tm.reshape(steps, nseq, D_MODEL), 0, 1)
    new_conv_s = jnp.swapaxes(cnew_tm, 0, 1)[None]
    new_k_s = nk.reshape(nseq, N_KV, HEAD_DIM, WINDOW).transpose(0, 3, 1, 2)[None]
    new_v_s = nv.reshape(nseq, N_KV, HEAD_DIM, WINDOW).transpose(0, 3, 1, 2)[None]

    return (y_prompt, y_sample, new_h_p, new_conv_p, new_k_p, new_v_p,
            hl_s[None], new_conv_s, new_k_s, new_v_s)
```

```python
import functools

import jax
import jax.numpy as jnp
from jax import lax
from jax.experimental import pallas as pl
from jax.experimental.pallas import tpu as pltpu

D_MODEL = 1024
D_RNN = 1024
D_PLE = 256
N_LRU_BLOCKS = 8
LRU_BLOCK = 128
CONV_W = 4
LRU_C = 8.0
EPS = 1e-6
HEAD_DIM = 64
N_HEADS = 16
N_KV = 4
GROUP = 4
Q_W = 1024
KV_W = 256
WINDOW = 128
BLOCK = 128
NEG_INF = -1e30
QK_SCALE = HEAD_DIM ** -0.5

LANES = 128
SUBLANES = 8
VMEM_LIMIT = 56 * 1024 * 1024

PROMPT_TILE = 256
SAMPLE_SEQS = 8

F32 = jnp.float32
BF16 = jnp.bfloat16


def _rms(x, g):
    ms = jnp.mean(x * x, axis=-1, keepdims=True)
    return x * lax.rsqrt(ms + EPS) * g


def _sigmoid(x):
    return 0.5 * jnp.tanh(0.5 * x) + 0.5


def _silu(x):
    return x * _sigmoid(x)


def _dot(a, w):
    return jnp.dot(a.astype(BF16), w, preferred_element_type=F32)


def _softplus(z):
    return jnp.maximum(z, 0.0) + jnp.log1p(jnp.exp(-jnp.abs(z)))


def _lru_coeffs(xc, wg_ref, b_r, b_i, lam):
    xcb = xc.astype(BF16)
    zr, zi = [], []
    for n in range(N_LRU_BLOCKS):
        z = jnp.dot(xcb[:, n * LRU_BLOCK:(n + 1) * LRU_BLOCK], wg_ref[n], preferred_element_type=F32)
        zr.append(z[:, :LRU_BLOCK])
        zi.append(z[:, LRU_BLOCK:])
    r = _sigmoid(jnp.concatenate(zr, axis=1) + b_r)
    i = _sigmoid(jnp.concatenate(zi, axis=1) + b_i)
    log_a = (-LRU_C * _softplus(-lam)) * r
    a = jnp.exp(log_a)
    b = jnp.sqrt(-jnp.tanh(log_a) * (a * a + 1.0)) * (i * xc)
    return a, b


def _ple(h, p, g, wpg_ref, wpp_ref):
    gate = _sigmoid(_dot(_rms(h, g), wpg_ref[...]))
    return h + gate * _dot(p, wpp_ref[...])


def _scan_rows(a, b, h0):
    t_rows, width = a.shape
    row = lax.broadcasted_iota(jnp.int32, (SUBLANES, width), 0)
    masks = [(s, row >= s) for s in (1, 2, 4)]
    carry = jnp.broadcast_to(h0, (SUBLANES, width))
    out = []
    for g in range(t_rows // SUBLANES):
        ag = a[g * SUBLANES:(g + 1) * SUBLANES]
        bg = b[g * SUBLANES:(g + 1) * SUBLANES]
        for s, m in masks:
            a_prev = pltpu.roll(ag, s, 0)
            b_prev = pltpu.roll(bg, s, 0)
            bg = bg + jnp.where(m, ag, 0.0) * b_prev
            ag = jnp.where(m, ag * a_prev, ag)
        hg = bg + ag * carry
        carry = jnp.broadcast_to(hg[SUBLANES - 1:SUBLANES], (SUBLANES, width))
        out.append(hg)
    return jnp.concatenate(out, axis=0)


def _shift_rows(x, prev8, k):
    row = lax.broadcasted_iota(jnp.int32, (SUBLANES, x.shape[1]), 0)
    first = row < k
    pieces = []
    prev = pltpu.roll(prev8, k, 0)
    for g in range(x.shape[0] // SUBLANES):
        cur = pltpu.roll(x[g * SUBLANES:(g + 1) * SUBLANES], k, 0)
        pieces.append(jnp.where(first, prev, cur))
        prev = cur
    return jnp.concatenate(pieces, axis=0)


V0_NORM, V0_PLE, V0_CB, V0_BR, V0_BI, V0_LAM, V0_CW = 0, 1, 2, 3, 4, 5, 6
V1_NORM, V1_PLE, V1_FINAL = 0, 1, 2


def _prompt_l0_kernel(x_ref, p_ref, vec_ref, win_ref, wg_ref, wout_ref, wpg_ref, wpp_ref,
                      h1_ref, hl_ref, tail_ref, prev8, hcar):
    t_rows = x_ref.shape[1]

    @pl.when(pl.program_id(0) == 0)
    def _():
        prev8[...] = jnp.zeros_like(prev8)
        hcar[...] = jnp.zeros_like(hcar)

    streams = range(x_ref.shape[0])
    xg = [_dot(_rms(x_ref[n], vec_ref[V0_NORM:V0_NORM + 1]), win_ref[...]) for n in streams]
    pp = [_dot(p_ref[n], wpp_ref[...]) for n in streams]
    xc = []
    for n in streams:
        xb = xg[n][:, :D_RNN]
        p8 = prev8[n]
        acc = vec_ref[V0_CB:V0_CB + 1] + vec_ref[V0_CW + 3:V0_CW + 4] * xb
        for k in range(1, CONV_W):
            acc = acc + vec_ref[V0_CW + 3 - k:V0_CW + 4 - k] * _shift_rows(xb, p8, k)
        xc.append(acc)
        prev8[n] = xb[t_rows - SUBLANES:]
        tail_ref[n] = xb[t_rows - SUBLANES:]
    ab = [_lru_coeffs(xc[n], wg_ref, vec_ref[V0_BR:V0_BR + 1], vec_ref[V0_BI:V0_BI + 1],
                      vec_ref[V0_LAM:V0_LAM + 1]) for n in streams]
    h1 = []
    for n in streams:
        h = _scan_rows(ab[n][0], ab[n][1], hcar[n])
        hcar[n] = h[t_rows - 1:]
        hl_ref[n] = h[t_rows - SUBLANES:]
        h1.append(x_ref[n] + _dot(h * _silu(xg[n][:, D_RNN:]), wout_ref[...]))
    gl = [_dot(_rms(h1[n], vec_ref[V0_PLE:V0_PLE + 1]), wpg_ref[...]) for n in streams]
    for n in streams:
        h1_ref[n] = h1[n] + _sigmoid(gl[n]) * pp[n]


def _const_spec(shape):
    return pl.BlockSpec(shape, lambda *_: (0,) * len(shape), pipeline_mode=pl.Buffered(1))


def _prompt_l0(x, p_all, vec0, w_in_a, wg, w_out_a, wpg0, wpp0):
    nb, seq, _ = x.shape
    t = PROMPT_TILE
    return pl.pallas_call(
        _prompt_l0_kernel,
        grid=(seq // t,),
        in_specs=[
            pl.BlockSpec((nb, t, D_MODEL), lambda s: (0, s, 0)),
            pl.BlockSpec((None, nb, t, D_PLE), lambda s: (0, 0, s, 0)),
            _const_spec(vec0.shape), _const_spec(w_in_a.shape), _const_spec(wg.shape),
            _const_spec(w_out_a.shape), _const_spec(wpg0.shape), _const_spec(wpp0.shape),
        ],
        out_specs=[
            pl.BlockSpec((nb, t, D_MODEL), lambda s: (0, s, 0)),
            pl.BlockSpec((nb, SUBLANES, D_RNN), lambda s: (0, 0, 0)),
            pl.BlockSpec((nb, SUBLANES, D_RNN), lambda s: (0, 0, 0)),
        ],
        out_shape=[
            jax.ShapeDtypeStruct((nb, seq, D_MODEL), F32),
            jax.ShapeDtypeStruct((nb, SUBLANES, D_RNN), F32),
            jax.ShapeDtypeStruct((nb, SUBLANES, D_RNN), F32),
        ],
        scratch_shapes=[pltpu.VMEM((nb, SUBLANES, D_RNN), F32), pltpu.VMEM((nb, 1, D_RNN), F32)],
        compiler_params=pltpu.CompilerParams(
            dimension_semantics=("arbitrary",), vmem_limit_bytes=VMEM_LIMIT),
        name="prompt_l0",
    )(x, p_all, vec0, w_in_a, wg, w_out_a, wpg0, wpp0)


def _slope(head):
    return 2.0 ** (-8.0 * (head + 1) / N_HEADS)


def _build_prompt_bias(bias_ref):
    shape = (BLOCK, 4 * BLOCK)
    qi = lax.broadcasted_iota(jnp.int32, shape, 0)
    col = lax.broadcasted_iota(jnp.int32, shape, 1)
    si = col & (2 * BLOCK - 1)
    second = col >= 2 * BLOCK
    dist = BLOCK + qi - si
    valid = (dist >= 0) & (dist < WINDOW)
    valid_first = valid & (si >= BLOCK)
    distf = dist.astype(F32)
    for hh in range(GROUP):
        for i in range(N_KV // 2):
            slope = jnp.where(second, _slope(GROUP * (2 * i + 1) + hh), _slope(GROUP * (2 * i) + hh))
            b = -(slope * distf)
            bias_ref[0, hh * 2 + i] = jnp.where(valid_first, b, NEG_INF)
            bias_ref[1, hh * 2 + i] = jnp.where(valid, b, NEG_INF)


def _masked_pair(x2, lo):
    return jnp.concatenate([jnp.where(lo, x2, 0.0), jnp.where(lo, 0.0, x2)], axis=0).astype(BF16)


def _attn_scores(q, k_prev, k_cur, bias_ref, tbl, lo):
    scores = []
    for i in range(N_KV // 2):
        lanes = slice(i * LANES, (i + 1) * LANES)
        k_both = _masked_pair(jnp.concatenate([k_prev[:, lanes], k_cur[:, lanes]], axis=0), lo)
        for hh in range(GROUP):
            q2 = q[:, hh * KV_W + i * LANES:hh * KV_W + (i + 1) * LANES]
            s2 = lax.dot_general(q2, k_both, (((1,), (1,)), ((), ())), preferred_element_type=F32)
            scores.append(s2 + bias_ref[tbl, hh * 2 + i])
    return scores


def _attn_values(scores, v_prev, v_cur, sink_ref, lo):
    cols = [None] * (2 * GROUP)
    for i in range(N_KV // 2):
        lanes = slice(i * LANES, (i + 1) * LANES)
        v_both = _masked_pair(jnp.concatenate([v_prev[:, lanes], v_cur[:, lanes]], axis=0), lo)
        for hh in range(GROUP):
            s2 = scores[i * GROUP + hh]
            es, inv = [], []
            for half in range(2):
                sink = sink_ref[0, hh * N_KV + 2 * i + half]
                sh = s2[:, half * 2 * BLOCK:(half + 1) * 2 * BLOCK]
                m = jnp.maximum(jnp.max(sh, axis=-1, keepdims=True), sink)
                e = jnp.exp(sh - m)
                denom = jnp.sum(e, axis=-1, keepdims=True) + jnp.exp(sink - m)
                es.append(e.astype(BF16))
                inv.append(1.0 / denom)
            o2 = jnp.dot(jnp.concatenate(es, axis=1), v_both, preferred_element_type=F32)
            cols[hh * 2 + i] = o2 * jnp.where(lo, inv[0], inv[1])
    return jnp.concatenate(cols, axis=1)


def _prompt_l1_kernel(sink_ref, h1_ref, p_ref, vec_ref, win_ref, wout_ref, wpg_ref, wpp_ref,
                      y_ref, kl_ref, vl_ref, kprev, vprev, bias_ref):
    nstreams, t_rows = h1_ref.shape[0], h1_ref.shape[1]
    nblk = t_rows // BLOCK
    s_idx = pl.program_id(0)

    @pl.when(s_idx == 0)
    def _():
        _build_prompt_bias(bias_ref)
        kprev[...] = jnp.zeros_like(kprev)
        vprev[...] = jnp.zeros_like(vprev)

    lo = lax.broadcasted_iota(jnp.int32, (1, LANES), 1) < HEAD_DIM
    first_tbl = jnp.where(s_idx == 0, 0, 1)
    streams = range(nstreams)
    proj = [_dot(_rms(h1_ref[n], vec_ref[V1_NORM:V1_NORM + 1]), win_ref[...]) for n in streams]
    pp = [_dot(p_ref[n], wpp_ref[...]) for n in streams]

    def kv_block(n, j, base, carry_ref):
        if j < 0:
            return carry_ref[n]
        return proj[n][j * BLOCK:(j + 1) * BLOCK, base:base + KV_W]

    def scores_of(n, j):
        q = (proj[n][j * BLOCK:(j + 1) * BLOCK, :Q_W] * QK_SCALE).astype(BF16)
        return _attn_scores(q, kv_block(n, j - 1, Q_W, kprev), kv_block(n, j, Q_W, kprev), bias_ref,
                            first_tbl if j == 0 else 1, lo)

    def finish(n, o_blocks):
        k_last = kv_block(n, nblk - 1, Q_W, kprev)
        v_last = kv_block(n, nblk - 1, Q_W + KV_W, vprev)
        kprev[n] = k_last
        vprev[n] = v_last
        kl_ref[n] = k_last
        vl_ref[n] = v_last
        gate = proj[n][:, Q_W + 2 * KV_W:]
        h2 = h1_ref[n] + _dot(jnp.concatenate(o_blocks, axis=0) * _silu(gate), wout_ref[...])
        gl = _dot(_rms(h2, vec_ref[V1_PLE:V1_PLE + 1]), wpg_ref[...])
        h2 = h2 + _sigmoid(gl) * pp[n]
        y_ref[n] = _rms(h2, vec_ref[V1_FINAL:V1_FINAL + 1])

    order = [(n, j) for n in streams for j in range(nblk)]
    o_blocks = {n: [] for n in streams}
    pending = None
    for unit in order + [None]:
        scores = scores_of(*unit) if unit is not None else None
        if pending is not None:
            (n, j), sc = pending
            o_blocks[n].append(_attn_values(sc, kv_block(n, j - 1, Q_W + KV_W, vprev),
                                            kv_block(n, j, Q_W + KV_W, vprev), sink_ref, lo))
            if j == nblk - 1:
                finish(n, o_blocks[n])
        pending = (unit, scores)


def _prompt_l1(sinks_p, h1, p_all, vec1, w_in_b, w_out_b, wpg1, wpp1):
    nb, seq, _ = h1.shape
    t = PROMPT_TILE
    return pl.pallas_call(
        _prompt_l1_kernel,
        grid=(seq // t,),
        in_specs=[
            pl.BlockSpec(memory_space=pltpu.SMEM),
            pl.BlockSpec((nb, t, D_MODEL), lambda s: (0, s, 0)),
            pl.BlockSpec((None, nb, t, D_PLE), lambda s: (1, 0, s, 0)),
            _const_spec(vec1.shape), _const_spec(w_in_b.shape), _const_spec(w_out_b.shape),
            _const_spec(wpg1.shape), _const_spec(wpp1.shape),
        ],
        out_specs=[
            pl.BlockSpec((nb, t, D_MODEL), lambda s: (0, s, 0)),
            pl.BlockSpec((nb, BLOCK, KV_W), lambda s: (0, 0, 0)),
            pl.BlockSpec((nb, BLOCK, KV_W), lambda s: (0, 0, 0)),
        ],
        out_shape=[
            jax.ShapeDtypeStruct((nb, seq, D_MODEL), F32),
            jax.ShapeDtypeStruct((nb, BLOCK, KV_W), F32),
            jax.ShapeDtypeStruct((nb, BLOCK, KV_W), F32),
        ],
        scratch_shapes=[
            pltpu.VMEM((nb, BLOCK, KV_W), F32), pltpu.VMEM((nb, BLOCK, KV_W), F32),
            pltpu.VMEM((2, 2 * GROUP, BLOCK, 4 * BLOCK), F32),
        ],
        compiler_params=pltpu.CompilerParams(
            dimension_semantics=("arbitrary",), vmem_limit_bytes=VMEM_LIMIT),
        name="prompt_l1",
    )(sinks_p, h1, p_all, vec1, w_in_b, w_out_b, wpg1, wpp1)


def _sample_l0_kernel(x_ref, p_ref, h0_ref, cst_ref, vec_ref, vec1_ref, win_ref, wg_ref, wout_ref, wpg_ref, wpp_ref,
                      winb_ref, h1_ref, hl_ref, cnew_ref, q_ref, k_ref, v_ref, gate_ref):
    nseq = h0_ref.shape[0]
    steps = x_ref.shape[0] // nseq
    x = x_ref[...]
    xg = _dot(_rms(x, vec_ref[V0_NORM:V0_NORM + 1]), win_ref[...])
    xb = xg[:, :D_RNN]
    gate = xg[:, D_RNN:]
    hist = [cst_ref[k] for k in range(CONV_W - 1)] + [xb[t * nseq:(t + 1) * nseq] for t in range(steps)]
    xc = []
    for t in range(steps):
        acc = vec_ref[V0_CB:V0_CB + 1] + vec_ref[V0_CW:V0_CW + 1] * hist[t]
        for k in range(1, CONV_W):
            acc = acc + vec_ref[V0_CW + k:V0_CW + k + 1] * hist[t + k]
        xc.append(acc)
    for k in range(CONV_W - 1):
        cnew_ref[k] = hist[steps + k]
    a, b = _lru_coeffs(jnp.concatenate(xc, axis=0), wg_ref, vec_ref[V0_BR:V0_BR + 1], vec_ref[V0_BI:V0_BI + 1],
                       vec_ref[V0_LAM:V0_LAM + 1])
    h = h0_ref[...]
    hs = []
    for t in range(steps):
        h = a[t * nseq:(t + 1) * nseq] * h + b[t * nseq:(t + 1) * nseq]
        hs.append(h)
    hl_ref[...] = h
    h1 = x + _dot(jnp.concatenate(hs, axis=0) * _silu(gate), wout_ref[...])
    h1 = _ple(h1, p_ref[...], vec_ref[V0_PLE:V0_PLE + 1], wpg_ref, wpp_ref)
    h1_ref[...] = h1
    proj = _dot(_rms(h1, vec1_ref[V1_NORM:V1_NORM + 1]), winb_ref[...])
    q_ref[...] = proj[:, :Q_W] * QK_SCALE
    k_ref[...] = proj[:, Q_W:Q_W + KV_W]
    v_ref[...] = proj[:, Q_W + KV_W:Q_W + 2 * KV_W]
    gate_ref[...] = proj[:, Q_W + 2 * KV_W:]


def _sample_l0(x_tm, p0_tm, h0, cst_tm, vec0, vec1, w_in_a, wg, w_out_a, wpg0, wpp0, w_in_b):
    rows = x_tm.shape[0]
    nseq = h0.shape[0]
    args = (x_tm, p0_tm, h0, cst_tm, vec0, vec1, w_in_a, wg, w_out_a, wpg0, wpp0, w_in_b)
    out_shapes = [(rows, D_MODEL), (nseq, D_RNN), (CONV_W - 1, nseq, D_RNN),
                  (rows, Q_W), (rows, KV_W), (rows, KV_W), (rows, Q_W)]
    return pl.pallas_call(
        _sample_l0_kernel,
        grid=(1,),
        in_specs=[_const_spec(a.shape) for a in args],
        out_specs=[pl.BlockSpec(s, lambda *_, n=len(s): (0,) * n) for s in out_shapes],
        out_shape=[jax.ShapeDtypeStruct(s, F32) for s in out_shapes],
        compiler_params=pltpu.CompilerParams(dimension_semantics=("arbitrary",), vmem_limit_bytes=VMEM_LIMIT),
        name="sample_l0",
    )(*args)


def _sample_attn_kernel(q_ref, knt_ref, vnt_ref, kc_ref, vc_ref, rowinfo_ref, o_ref, nk_ref, nv_ref):
    nseq = q_ref.shape[0]
    nq = q_ref.shape[1]
    steps = nq // GROUP
    rows = N_KV * nq
    sink = rowinfo_ref[:, 0:1]
    slope = rowinfo_ref[:, 1:2]
    tq = rowinfo_ref[:, 2:3]
    col = lax.broadcasted_iota(jnp.int32, (rows, 2 * WINDOW), 1).astype(F32)
    dist = WINDOW + tq - col
    valid = (dist >= 0.0) & (dist < WINDOW)
    bias = jnp.where(valid, -(slope * dist), NEG_INF)
    lane_grp = lax.broadcasted_iota(jnp.int32, (1, KV_W), 1) // HEAD_DIM
    new_lane = lax.broadcasted_iota(jnp.int32, (1, WINDOW), 1) < steps
    knt_all = knt_ref[...]
    vnt_all = vnt_ref[...]

    for bl in range(nseq):
        q = q_ref[bl]
        q_full = jnp.concatenate([jnp.where(lane_grp == g, q, 0.0) for g in range(N_KV)], axis=0).astype(BF16)
        shift = (LANES - steps * bl) % LANES
        knt = knt_all if shift == 0 else pltpu.roll(knt_all, shift, 1)
        vnt = vnt_all if shift == 0 else pltpu.roll(vnt_all, shift, 1)
        kt_ext = jnp.concatenate([kc_ref[bl], jnp.where(new_lane, knt, 0.0)], axis=1)
        vt_ext = jnp.concatenate([vc_ref[bl], jnp.where(new_lane, vnt, 0.0)], axis=1)
        s = jnp.dot(q_full, kt_ext.astype(BF16), preferred_element_type=F32) + bias
        m = jnp.maximum(jnp.max(s, axis=-1, keepdims=True), sink)
        e = jnp.exp(s - m)
        denom = jnp.sum(e, axis=-1, keepdims=True) + jnp.exp(sink - m)
        o = lax.dot_general(e.astype(BF16), vt_ext.astype(BF16), (((1,), (1,)), ((), ())),
                            preferred_element_type=F32) * (1.0 / denom)
        acc = jnp.where(lane_grp == 0, o[:nq], 0.0)
        for g in range(1, N_KV):
            acc = acc + jnp.where(lane_grp == g, o[g * nq:(g + 1) * nq], 0.0)
        o_ref[bl] = acc
        nk_ref[bl] = pltpu.roll(kt_ext, 2 * WINDOW - steps, 1)[:, :WINDOW]
        nv_ref[bl] = pltpu.roll(vt_ext, 2 * WINDOW - steps, 1)[:, :WINDOW]


def _sample_attn(q16, knt, vnt, kc, vc, rowinfo):
    nseq_total, nq, _ = q16.shape
    sb = SAMPLE_SEQS
    blk3 = lambda i: (i, 0, 0)
    return pl.pallas_call(
        _sample_attn_kernel,
        grid=(nseq_total // sb,),
        in_specs=[
            pl.BlockSpec((sb, nq, KV_W), blk3),
            pl.BlockSpec((None, KV_W, LANES), blk3),
            pl.BlockSpec((None, KV_W, LANES), blk3),
            pl.BlockSpec((sb, KV_W, WINDOW), blk3),
            pl.BlockSpec((sb, KV_W, WINDOW), blk3),
            pl.BlockSpec(rowinfo.shape, lambda i: (0, 0)),
        ],
        out_specs=[
            pl.BlockSpec((sb, nq, KV_W), blk3),
            pl.BlockSpec((sb, KV_W, WINDOW), blk3),
            pl.BlockSpec((sb, KV_W, WINDOW), blk3),
        ],
        out_shape=[
            jax.ShapeDtypeStruct((nseq_total, nq, KV_W), F32),
            jax.ShapeDtypeStruct((nseq_total, KV_W, WINDOW), F32),
            jax.ShapeDtypeStruct((nseq_total, KV_W, WINDOW), F32),
        ],
        compiler_params=pltpu.CompilerParams(dimension_semantics=("arbitrary",), vmem_limit_bytes=VMEM_LIMIT),
        name="sample_attn",
    )(q16, knt, vnt, kc, vc, rowinfo)


def _sample_tail_kernel(o_ref, gate_ref, h1_ref, p_ref, vec_ref, wout_ref, wpg_ref, wpp_ref, y_ref):
    o = jnp.concatenate([o_ref[hh] for hh in range(GROUP)], axis=1)
    h1 = h1_ref[...]
    h2 = h1 + _dot(o * _silu(gate_ref[...]), wout_ref[...])
    h2 = _ple(h2, p_ref[...], vec_ref[V1_PLE:V1_PLE + 1], wpg_ref, wpp_ref)
    y_ref[...] = _rms(h2, vec_ref[V1_FINAL:V1_FINAL + 1])


def _sample_tail(o4, gate, h1, p1_tm, vec1, w_out_b, wpg1, wpp1):
    args = (o4, gate, h1, p1_tm, vec1, w_out_b, wpg1, wpp1)
    return pl.pallas_call(
        _sample_tail_kernel,
        grid=(1,),
        in_specs=[_const_spec(a.shape) for a in args],
        out_specs=pl.BlockSpec(h1.shape, lambda i: (0, 0)),
        out_shape=jax.ShapeDtypeStruct(h1.shape, F32),
        compiler_params=pltpu.CompilerParams(dimension_semantics=("arbitrary",), vmem_limit_bytes=VMEM_LIMIT),
        name="sample_tail",
    )(*args)


def _heads_to_slot_major(w, axis):
    shape = w.shape
    w = w.reshape(shape[:axis] + (N_KV, GROUP, HEAD_DIM) + shape[axis + 1:])
    w = jnp.swapaxes(w, axis, axis + 1)
    return w.reshape(shape)


def _pad_rows(v, rows):
    return jnp.concatenate([v, jnp.zeros((rows - v.shape[0], v.shape[1]), v.dtype)], axis=0)


def kernel(x_prompt, x_sample, p_prompt, p_sample, state_lru_h, state_conv, cache_k, cache_v, norm_g, final_norm_g, ple_norm_g, w_ple_gate, w_ple_proj, w_in_a, conv_w_a, conv_b_a, w_rgate, b_rgate, w_igate, b_igate, lru_lambda, w_out_a, w_in_b, sinks, w_out_b):
    assert norm_g.shape[0] == 2 and w_in_a.shape[0] == 1 and w_in_b.shape[0] == 1
    nseq, steps, _ = x_sample.shape

    vec0 = _pad_rows(jnp.concatenate([norm_g[0:1], ple_norm_g[0:1], conv_b_a, b_rgate, b_igate, lru_lambda,
                                      conv_w_a[0]], axis=0), 2 * SUBLANES)
    vec1 = _pad_rows(jnp.concatenate([norm_g[1:2], ple_norm_g[1:2], final_norm_g[None]], axis=0), SUBLANES)
    w_in_a_b = w_in_a[0].astype(BF16)
    w_out_a_b = w_out_a[0].astype(BF16)
    wg = jnp.concatenate([w_rgate[0], w_igate[0]], axis=-1).astype(BF16)
    wpg = w_ple_gate.astype(BF16)
    wpp = w_ple_proj.astype(BF16)
    wb = w_in_b[0]
    w_in_b_b = jnp.concatenate([_heads_to_slot_major(wb[:, :Q_W], 1), wb[:, Q_W:Q_W + 2 * KV_W],
                                _heads_to_slot_major(wb[:, Q_W + 2 * KV_W:], 1)], axis=1).astype(BF16)
    w_out_b_b = _heads_to_slot_major(w_out_b[0], 0).astype(BF16)
    sinks_p = sinks[0].reshape(N_KV, GROUP).T.reshape(1, N_HEADS)

    h1_p, hl_p, tail_p = _prompt_l0(x_prompt, p_prompt, vec0, w_in_a_b, wg, w_out_a_b, wpg[0], wpp[0])
    y_prompt, kl_p, vl_p = _prompt_l1(sinks_p, h1_p, p_prompt, vec1, w_in_b_b, w_out_b_b, wpg[1], wpp[1])
    nbp = x_prompt.shape[0]
    new_h_p = hl_p[:, SUBLANES - 1][None]
    new_conv_p = tail_p[:, SUBLANES - (CONV_W - 1):][None]
    new_k_p = kl_p.reshape(1, nbp, WINDOW, N_KV, HEAD_DIM)
    new_v_p = vl_p.reshape(1, nbp, WINDOW, N_KV, HEAD_DIM)

    rows = steps * nseq
    x_tm = jnp.swapaxes(x_sample, 0, 1).reshape(rows, D_MODEL)
    p_tm = jnp.swapaxes(p_sample, 1, 2).reshape(2, rows, D_PLE)
    cst_tm = jnp.swapaxes(state_conv[0], 0, 1)
    h1_s, hl_s, cnew_tm, q_s, k_s, v_s, gate_s = _sample_l0(
        x_tm, p_tm[0], state_lru_h[0], cst_tm, vec0, vec1, w_in_a_b, wg, w_out_a_b, wpg[0], wpp[0], w_in_b_b)

    sb = SAMPLE_SEQS
    q16 = q_s.reshape(steps, nseq, GROUP, KV_W).transpose(1, 2, 0, 3).reshape(nseq, GROUP * steps, KV_W)

    def new_rows_t(a):
        a = a.reshape(steps, nseq // sb, sb, KV_W).transpose(1, 3, 2, 0).reshape(nseq // sb, KV_W, sb * steps)
        return jnp.concatenate([a, jnp.zeros((nseq // sb, KV_W, LANES - sb * steps), a.dtype)], axis=-1)

    kc = cache_k[0].transpose(0, 2, 3, 1).reshape(nseq, KV_W, WINDOW)
    vc = cache_v[0].transpose(0, 2, 3, 1).reshape(nseq, KV_W, WINDOW)
    head = jnp.arange(N_KV * GROUP * steps) // steps
    slopes = jnp.exp2(-8.0 * (head + 1).astype(F32) / N_HEADS)
    tq = (jnp.arange(N_KV * GROUP * steps) % steps).astype(F32)
    rowinfo = jnp.stack([sinks[0][head], slopes, tq], axis=1)
    rowinfo = jnp.concatenate([rowinfo, jnp.zeros((rowinfo.shape[0], LANES - 3), F32)], axis=1)
    o16, nk, nv = _sample_attn(q16, new_rows_t(k_s), new_rows_t(v_s), kc, vc, rowinfo)

    o4 = o16.reshape(nseq, GROUP, steps, KV_W).transpose(1, 2, 0, 3).reshape(GROUP, rows, KV_W)
    y_tm = _sample_tail(o4, gate_s, h1_s, p_tm[1], vec1, w_out_b_b, wpg[1], wpp[1])
    y_sample = jnp.swapaxes(y_tm.reshape(steps, nseq, D_MODEL), 0, 1)
    new_conv_s = jnp.swapaxes(cnew_tm, 0, 1)[None]
    new_k_s = nk.reshape(nseq, N_KV, HEAD_DIM, WINDOW).transpose(0, 3, 1, 2)[None]
    new_v_s = nv.reshape(nseq, N_KV, HEAD_DIM, WINDOW).transpose(0, 3, 1, 2)[None]

    return (y_prompt, y_sample, new_h_p, new_conv_p, new_k_p, new_v_p,
            hl_s[None], new_conv_s, new_k_s, new_v_s)
```

```python
import functools

import jax
import jax.numpy as jnp
from jax import lax
from jax.experimental import pallas as pl
from jax.experimental.pallas import tpu as pltpu

D_MODEL = 1024
D_RNN = 1024
D_PLE = 256
N_LRU_BLOCKS = 8
LRU_BLOCK = 128
CONV_W = 4
LRU_C = 8.0
EPS = 1e-6
HEAD_DIM = 64
N_HEADS = 16
N_KV = 4
GROUP = 4
Q_W = 1024
KV_W = 256
WINDOW = 128
BLOCK = 128
NEG_INF = -1e30
QK_SCALE = HEAD_DIM ** -0.5
SQRT_FLOOR = 1e-37

LANES = 128
SUBLANES = 8
VMEM_LIMIT = 56 * 1024 * 1024

PROMPT_TILE = 256
SAMPLE_SEQS = 8

F32 = jnp.float32
BF16 = jnp.bfloat16


def _rms(x, g):
    ms = jnp.mean(x * x, axis=-1, keepdims=True)
    return x * lax.rsqrt(ms + EPS) * g


def _sigmoid(x):
    return 0.5 * jnp.tanh(0.5 * x) + 0.5


def _silu(x):
    return x * _sigmoid(x)


def _dot(a, w):
    return jnp.dot(a.astype(BF16), w, preferred_element_type=F32)


def _softplus(z):
    return jnp.maximum(z, 0.0) + jnp.log1p(jnp.exp(-jnp.abs(z)))


def _lru_gate_logits(xc, wg_ref):
    xcb = xc.astype(BF16)
    zr, zi = [], []
    for n in range(N_LRU_BLOCKS):
        z = jnp.dot(xcb[:, n * LRU_BLOCK:(n + 1) * LRU_BLOCK], wg_ref[n], preferred_element_type=F32)
        zr.append(z[:, :LRU_BLOCK])
        zi.append(z[:, LRU_BLOCK:])
    return jnp.concatenate(zr, axis=1), jnp.concatenate(zi, axis=1)


def _lru_coeffs(xc, logits, b_r, b_i, lam):
    r = _sigmoid(logits[0] + b_r)
    i = _sigmoid(logits[1] + b_i)
    log_a = (-LRU_C * _softplus(-lam)) * r
    a = jnp.exp(log_a)
    y = -jnp.tanh(log_a) * (a * a + 1.0)
    b = (y * lax.rsqrt(jnp.maximum(y, SQRT_FLOOR))) * (i * xc)
    return a, b


def _ple(h, p, g, wpg_ref, wpp_ref):
    gate = _sigmoid(_dot(_rms(h, g), wpg_ref[...]))
    return h + gate * _dot(p, wpp_ref[...])


def _scan_rows(a, b, h0):
    t_rows, width = a.shape
    row = lax.broadcasted_iota(jnp.int32, (SUBLANES, width), 0)
    masks = [(s, row >= s) for s in (1, 2, 4)]
    carry = jnp.broadcast_to(h0, (SUBLANES, width))
    out = []
    for g in range(t_rows // SUBLANES):
        ag = a[g * SUBLANES:(g + 1) * SUBLANES]
        bg = b[g * SUBLANES:(g + 1) * SUBLANES]
        for s, m in masks:
            a_prev = pltpu.roll(ag, s, 0)
            b_prev = pltpu.roll(bg, s, 0)
            bg = bg + jnp.where(m, ag, 0.0) * b_prev
            ag = jnp.where(m, ag * a_prev, ag)
        hg = bg + ag * carry
        carry = jnp.broadcast_to(hg[SUBLANES - 1:SUBLANES], (SUBLANES, width))
        out.append(hg)
    return jnp.concatenate(out, axis=0)


def _shift_rows(x, prev8, k):
    row = lax.broadcasted_iota(jnp.int32, (SUBLANES, x.shape[1]), 0)
    first = row < k
    pieces = []
    prev = pltpu.roll(prev8, k, 0)
    for g in range(x.shape[0] // SUBLANES):
        cur = pltpu.roll(x[g * SUBLANES:(g + 1) * SUBLANES], k, 0)
        pieces.append(jnp.where(first, prev, cur))
        prev = cur
    return jnp.concatenate(pieces, axis=0)


V0_NORM, V0_PLE, V0_CB, V0_BR, V0_BI, V0_LAM, V0_CW = 0, 1, 2, 3, 4, 5, 6
V1_NORM, V1_PLE, V1_FINAL = 0, 1, 2


def _stage_weights(srcs, dsts):
    for src, dst in zip(srcs, dsts):
        dst[...] = src[...]


def _prompt_l0_kernel(x_ref, p_ref, vec_ref, win_in, wg_in, wout_in, wpg_in, wpp_in,
                      h1_ref, hl_ref, tail_ref, prev8, hcar, win_ref, wg_ref, wout_ref, wpg_ref, wpp_ref):
    t_rows = x_ref.shape[1]

    @pl.when(pl.program_id(0) == 0)
    def _():
        prev8[...] = jnp.zeros_like(prev8)
        hcar[...] = jnp.zeros_like(hcar)
        _stage_weights((win_in, wg_in, wout_in, wpg_in, wpp_in), (win_ref, wg_ref, wout_ref, wpg_ref, wpp_ref))

    st = [dict() for _ in range(x_ref.shape[0])]

    def in_proj(n):
        st[n]["xg"] = _dot(_rms(x_ref[n], vec_ref[V0_NORM:V0_NORM + 1]), win_ref[...])
        st[n]["pp"] = _dot(p_ref[n], wpp_ref[...])

    def conv_gates(n):
        xb = st[n]["xg"][:, :D_RNN]
        p8 = prev8[n]
        acc = vec_ref[V0_CB:V0_CB + 1] + vec_ref[V0_CW + 3:V0_CW + 4] * xb
        for k in range(1, CONV_W):
            acc = acc + vec_ref[V0_CW + 3 - k:V0_CW + 4 - k] * _shift_rows(xb, p8, k)
        prev8[n] = xb[t_rows - SUBLANES:]
        tail_ref[n] = xb[t_rows - SUBLANES:]
        st[n]["xc"] = acc
        st[n]["logits"] = _lru_gate_logits(acc, wg_ref)

    def recurrence(n):
        a, b = _lru_coeffs(st[n]["xc"], st[n]["logits"], vec_ref[V0_BR:V0_BR + 1], vec_ref[V0_BI:V0_BI + 1],
                           vec_ref[V0_LAM:V0_LAM + 1])
        h = _scan_rows(a, b, hcar[n])
        hcar[n] = h[t_rows - 1:]
        hl_ref[n] = h[t_rows - SUBLANES:]
        st[n]["hg"] = h * _silu(st[n]["xg"][:, D_RNN:])

    def out_proj(n):
        h1 = x_ref[n] + _dot(st[n]["hg"], wout_ref[...])
        gl = _dot(_rms(h1, vec_ref[V0_PLE:V0_PLE + 1]), wpg_ref[...])
        h1_ref[n] = h1 + _sigmoid(gl) * st[n]["pp"]

    for stage in (in_proj, conv_gates, recurrence, out_proj):
        for n in range(len(st)):
            stage(n)


def _const_spec(shape):
    return pl.BlockSpec(shape, lambda *_: (0,) * len(shape), pipeline_mode=pl.Buffered(1))


def _prompt_l0(x, p_all, vec0, w_in_a, wg, w_out_a, wpg0, wpp0):
    nb, seq, _ = x.shape
    t = PROMPT_TILE
    return pl.pallas_call(
        _prompt_l0_kernel,
        grid=(seq // t,),
        in_specs=[
            pl.BlockSpec((nb, t, D_MODEL), lambda s: (0, s, 0)),
            pl.BlockSpec((None, nb, t, D_PLE), lambda s: (0, 0, s, 0)),
            _const_spec(vec0.shape), _const_spec(w_in_a.shape), _const_spec(wg.shape),
            _const_spec(w_out_a.shape), _const_spec(wpg0.shape), _const_spec(wpp0.shape),
        ],
        out_specs=[
            pl.BlockSpec((nb, t, D_MODEL), lambda s: (0, s, 0)),
            pl.BlockSpec((nb, SUBLANES, D_RNN), lambda s: (0, 0, 0)),
            pl.BlockSpec((nb, SUBLANES, D_RNN), lambda s: (0, 0, 0)),
        ],
        out_shape=[
            jax.ShapeDtypeStruct((nb, seq, D_MODEL), F32),
            jax.ShapeDtypeStruct((nb, SUBLANES, D_RNN), F32),
            jax.ShapeDtypeStruct((nb, SUBLANES, D_RNN), F32),
        ],
        scratch_shapes=[pltpu.VMEM((nb, SUBLANES, D_RNN), F32), pltpu.VMEM((nb, 1, D_RNN), F32)]
        + [pltpu.VMEM(w.shape, w.dtype) for w in (w_in_a, wg, w_out_a, wpg0, wpp0)],
        compiler_params=pltpu.CompilerParams(
            dimension_semantics=("arbitrary",), vmem_limit_bytes=VMEM_LIMIT),
        name="prompt_l0",
    )(x, p_all, vec0, w_in_a, wg, w_out_a, wpg0, wpp0)


def _slope(head):
    return 2.0 ** (-8.0 * (head + 1) / N_HEADS)


def _build_prompt_bias(bias_ref):
    shape = (BLOCK, 4 * BLOCK)
    qi = lax.broadcasted_iota(jnp.int32, shape, 0)
    col = lax.broadcasted_iota(jnp.int32, shape, 1)
    si = col & (2 * BLOCK - 1)
    second = col >= 2 * BLOCK
    dist = BLOCK + qi - si
    valid = (dist >= 0) & (dist < WINDOW)
    valid_first = valid & (si >= BLOCK)
    distf = dist.astype(F32)
    for hh in range(GROUP):
        for i in range(N_KV // 2):
            slope = jnp.where(second, _slope(GROUP * (2 * i + 1) + hh), _slope(GROUP * (2 * i) + hh))
            b = -(slope * distf)
            bias_ref[0, hh * 2 + i] = jnp.where(valid_first, b, NEG_INF)
            bias_ref[1, hh * 2 + i] = jnp.where(valid, b, NEG_INF)


def _masked_pair(x2, lo):
    return jnp.concatenate([jnp.where(lo, x2, 0.0), jnp.where(lo, 0.0, x2)], axis=0).astype(BF16)


def _attn_scores(q, k_prev, k_cur, bias_ref, tbl, lo):
    scores = []
    for i in range(N_KV // 2):
        lanes = slice(i * LANES, (i + 1) * LANES)
        k_both = _masked_pair(jnp.concatenate([k_prev[:, lanes], k_cur[:, lanes]], axis=0), lo)
        for hh in range(GROUP):
            q2 = q[:, hh * KV_W + i * LANES:hh * KV_W + (i + 1) * LANES]
            s2 = lax.dot_general(q2, k_both, (((1,), (1,)), ((), ())), preferred_element_type=F32)
            scores.append(s2 + bias_ref[tbl, hh * 2 + i])
    return scores


def _attn_values(scores, v_prev, v_cur, sink_ref, lo):
    cols = [None] * (2 * GROUP)
    for i in range(N_KV // 2):
        lanes = slice(i * LANES, (i + 1) * LANES)
        v_both = _masked_pair(jnp.concatenate([v_prev[:, lanes], v_cur[:, lanes]], axis=0), lo)
        for hh in range(GROUP):
            s2 = scores[i * GROUP + hh]
            es, inv = [], []
            for half in range(2):
                sink = sink_ref[0, hh * N_KV + 2 * i + half]
                sh = s2[:, half * 2 * BLOCK:(half + 1) * 2 * BLOCK]
                m = jnp.maximum(jnp.max(sh, axis=-1, keepdims=True), sink)
                e = jnp.exp(sh - m)
                denom = jnp.sum(e, axis=-1, keepdims=True) + jnp.exp(sink - m)
                es.append(e.astype(BF16))
                inv.append(1.0 / denom)
            o2 = jnp.dot(jnp.concatenate(es, axis=1), v_both, preferred_element_type=F32)
            cols[hh * 2 + i] = o2 * jnp.where(lo, inv[0], inv[1])
    return jnp.concatenate(cols, axis=1)


def _prompt_l1_kernel(sink_ref, h1_ref, p_ref, vec_ref, win_in, wout_in, wpg_in, wpp_in,
                      y_ref, kl_ref, vl_ref, kprev, vprev, bias_ref, win_ref, wout_ref, wpg_ref, wpp_ref):
    nstreams, t_rows = h1_ref.shape[0], h1_ref.shape[1]
    nblk = t_rows // BLOCK
    s_idx = pl.program_id(0)

    @pl.when(s_idx == 0)
    def _():
        _build_prompt_bias(bias_ref)
        kprev[...] = jnp.zeros_like(kprev)
        vprev[...] = jnp.zeros_like(vprev)
        _stage_weights((win_in, wout_in, wpg_in, wpp_in), (win_ref, wout_ref, wpg_ref, wpp_ref))

    lo = lax.broadcasted_iota(jnp.int32, (1, LANES), 1) < HEAD_DIM
    first_tbl = jnp.where(s_idx == 0, 0, 1)
    streams = range(nstreams)
    proj = [_dot(_rms(h1_ref[n], vec_ref[V1_NORM:V1_NORM + 1]), win_ref[...]) for n in streams]
    pp = [_dot(p_ref[n], wpp_ref[...]) for n in streams]

    def kv_block(n, j, base, carry_ref):
        if j < 0:
            return carry_ref[n]
        return proj[n][j * BLOCK:(j + 1) * BLOCK, base:base + KV_W]

    def scores_of(n, j):
        q = (proj[n][j * BLOCK:(j + 1) * BLOCK, :Q_W] * QK_SCALE).astype(BF16)
        return _attn_scores(q, kv_block(n, j - 1, Q_W, kprev), kv_block(n, j, Q_W, kprev), bias_ref,
                            first_tbl if j == 0 else 1, lo)

    def finish(n, o_blocks):
        k_last = kv_block(n, nblk - 1, Q_W, kprev)
        v_last = kv_block(n, nblk - 1, Q_W + KV_W, vprev)
        kprev[n] = k_last
        vprev[n] = v_last
        kl_ref[n] = k_last
        vl_ref[n] = v_last
        gate = proj[n][:, Q_W + 2 * KV_W:]
        h2 = h1_ref[n] + _dot(jnp.concatenate(o_blocks, axis=0) * _silu(gate), wout_ref[...])
        gl = _dot(_rms(h2, vec_ref[V1_PLE:V1_PLE + 1]), wpg_ref[...])
        h2 = h2 + _sigmoid(gl) * pp[n]
        y_ref[n] = _rms(h2, vec_ref[V1_FINAL:V1_FINAL + 1])

    order = [(n, j) for n in streams for j in range(nblk)]
    o_blocks = {n: [] for n in streams}
    pending = None
    for unit in order + [None]:
        scores = scores_of(*unit) if unit is not None else None
        if pending is not None:
            (n, j), sc = pending
            o_blocks[n].append(_attn_values(sc, kv_block(n, j - 1, Q_W + KV_W, vprev),
                                            kv_block(n, j, Q_W + KV_W, vprev), sink_ref, lo))
            if j == nblk - 1:
                finish(n, o_blocks[n])
        pending = (unit, scores)


def _prompt_l1(sinks_p, h1, p_all, vec1, w_in_b, w_out_b, wpg1, wpp1):
    nb, seq, _ = h1.shape
    t = PROMPT_TILE
    return pl.pallas_call(
        _prompt_l1_kernel,
        grid=(seq // t,),
        in_specs=[
            pl.BlockSpec(memory_space=pltpu.SMEM),
            pl.BlockSpec((nb, t, D_MODEL), lambda s: (0, s, 0)),
            pl.BlockSpec((None, nb, t, D_PLE), lambda s: (1, 0, s, 0)),
            _const_spec(vec1.shape), _const_spec(w_in_b.shape), _const_spec(w_out_b.shape),
            _const_spec(wpg1.shape), _const_spec(wpp1.shape),
        ],
        out_specs=[
            pl.BlockSpec((nb, t, D_MODEL), lambda s: (0, s, 0)),
            pl.BlockSpec((nb, BLOCK, KV_W), lambda s: (0, 0, 0)),
            pl.BlockSpec((nb, BLOCK, KV_W), lambda s: (0, 0, 0)),
        ],
        out_shape=[
            jax.ShapeDtypeStruct((nb, seq, D_MODEL), F32),
            jax.ShapeDtypeStruct((nb, BLOCK, KV_W), F32),
            jax.ShapeDtypeStruct((nb, BLOCK, KV_W), F32),
        ],
        scratch_shapes=[
            pltpu.VMEM((nb, BLOCK, KV_W), F32), pltpu.VMEM((nb, BLOCK, KV_W), F32),
            pltpu.VMEM((2, 2 * GROUP, BLOCK, 4 * BLOCK), F32),
        ] + [pltpu.VMEM(w.shape, w.dtype) for w in (w_in_b, w_out_b, wpg1, wpp1)],
        compiler_params=pltpu.CompilerParams(
            dimension_semantics=("arbitrary",), vmem_limit_bytes=VMEM_LIMIT),
        name="prompt_l1",
    )(sinks_p, h1, p_all, vec1, w_in_b, w_out_b, wpg1, wpp1)


def _sample_l0_kernel(x_ref, p_ref, h0_ref, cst_ref, vec_ref, vec1_ref, win_ref, wg_ref, wout_ref, wpg_ref, wpp_ref,
                      winb_ref, h1_ref, hl_ref, cnew_ref, q_ref, k_ref, v_ref, gate_ref):
    nseq = h0_ref.shape[0]
    steps = x_ref.shape[0] // nseq
    x = x_ref[...]
    xg = _dot(_rms(x, vec_ref[V0_NORM:V0_NORM + 1]), win_ref[...])
    xb = xg[:, :D_RNN]
    gate = xg[:, D_RNN:]
    hist = [cst_ref[k] for k in range(CONV_W - 1)] + [xb[t * nseq:(t + 1) * nseq] for t in range(steps)]
    xc = []
    for t in range(steps):
        acc = vec_ref[V0_CB:V0_CB + 1] + vec_ref[V0_CW:V0_CW + 1] * hist[t]
        for k in range(1, CONV_W):
            acc = acc + vec_ref[V0_CW + k:V0_CW + k + 1] * hist[t + k]
        xc.append(acc)
    for k in range(CONV_W - 1):
        cnew_ref[k] = hist[steps + k]
    xc = jnp.concatenate(xc, axis=0)
    a, b = _lru_coeffs(xc, _lru_gate_logits(xc, wg_ref), vec_ref[V0_BR:V0_BR + 1], vec_ref[V0_BI:V0_BI + 1],
                       vec_ref[V0_LAM:V0_LAM + 1])
    h = h0_ref[...]
    hs = []
    for t in range(steps):
        h = a[t * nseq:(t + 1) * nseq] * h + b[t * nseq:(t + 1) * nseq]
        hs.append(h)
    hl_ref[...] = h
    h1 = x + _dot(jnp.concatenate(hs, axis=0) * _silu(gate), wout_ref[...])
    h1 = _ple(h1, p_ref[...], vec_ref[V0_PLE:V0_PLE + 1], wpg_ref, wpp_ref)
    h1_ref[...] = h1
    proj = _dot(_rms(h1, vec1_ref[V1_NORM:V1_NORM + 1]), winb_ref[...])
    q_ref[...] = proj[:, :Q_W] * QK_SCALE
    k_ref[...] = proj[:, Q_W:Q_W + KV_W]
    v_ref[...] = proj[:, Q_W + KV_W:Q_W + 2 * KV_W]
    gate_ref[...] = proj[:, Q_W + 2 * KV_W:]


def _sample_l0(x_tm, p0_tm, h0, cst_tm, vec0, vec1, w_in_a, wg, w_out_a, wpg0, wpp0, w_in_b):
    rows = x_tm.shape[0]
    nseq = h0.shape[0]
    args = (x_tm, p0_tm, h0, cst_tm, vec0, vec1, w_in_a, wg, w_out_a, wpg0, wpp0, w_in_b)
    out_shapes = [(rows, D_MODEL), (nseq, D_RNN), (CONV_W - 1, nseq, D_RNN),
                  (rows, Q_W), (rows, KV_W), (rows, KV_W), (rows, Q_W)]
    return pl.pallas_call(
        _sample_l0_kernel,
        grid=(1,),
        in_specs=[_const_spec(a.shape) for a in args],
        out_specs=[pl.BlockSpec(s, lambda *_, n=len(s): (0,) * n) for s in out_shapes],
        out_shape=[jax.ShapeDtypeStruct(s, F32) for s in out_shapes],
        compiler_params=pltpu.CompilerParams(dimension_semantics=("arbitrary",), vmem_limit_bytes=VMEM_LIMIT),
        name="sample_l0",
    )(*args)


def _sample_attn_kernel(q_ref, knt_ref, vnt_ref, kc_ref, vc_ref, rowinfo_ref, o_ref, nk_ref, nv_ref):
    nseq = q_ref.shape[0]
    nq = q_ref.shape[1]
    steps = nq // GROUP
    rows = N_KV * nq
    sink = rowinfo_ref[:, 0:1]
    slope = rowinfo_ref[:, 1:2]
    tq = rowinfo_ref[:, 2:3]
    col = lax.broadcasted_iota(jnp.int32, (rows, 2 * WINDOW), 1).astype(F32)
    dist = WINDOW + tq - col
    valid = (dist >= 0.0) & (dist < WINDOW)
    bias = jnp.where(valid, -(slope * dist), NEG_INF)
    lane_grp = lax.broadcasted_iota(jnp.int32, (1, KV_W), 1) // HEAD_DIM
    new_lane = lax.broadcasted_iota(jnp.int32, (1, WINDOW), 1) < steps
    knt_all = knt_ref[...]
    vnt_all = vnt_ref[...]

    for bl in range(nseq):
        q = q_ref[bl]
        q_full = jnp.concatenate([jnp.where(lane_grp == g, q, 0.0) for g in range(N_KV)], axis=0).astype(BF16)
        shift = (LANES - steps * bl) % LANES
        knt = knt_all if shift == 0 else pltpu.roll(knt_all, shift, 1)
        vnt = vnt_all if shift == 0 else pltpu.roll(vnt_all, shift, 1)
        kt_ext = jnp.concatenate([kc_ref[bl], jnp.where(new_lane, knt, 0.0)], axis=1)
        vt_ext = jnp.concatenate([vc_ref[bl], jnp.where(new_lane, vnt, 0.0)], axis=1)
        s = jnp.dot(q_full, kt_ext.astype(BF16), preferred_element_type=F32) + bias
        m = jnp.maximum(jnp.max(s, axis=-1, keepdims=True), sink)
        e = jnp.exp(s - m)
        denom = jnp.sum(e, axis=-1, keepdims=True) + jnp.exp(sink - m)
        o = lax.dot_general(e.astype(BF16), vt_ext.astype(BF16), (((1,), (1,)), ((), ())),
                            preferred_element_type=F32) * (1.0 / denom)
        acc = jnp.where(lane_grp == 0, o[:nq], 0.0)
        for g in range(1, N_KV):
            acc = acc + jnp.where(lane_grp == g, o[g * nq:(g + 1) * nq], 0.0)
        o_ref[bl] = acc
        nk_ref[bl] = pltpu.roll(kt_ext, 2 * WINDOW - steps, 1)[:, :WINDOW]
        nv_ref[bl] = pltpu.roll(vt_ext, 2 * WINDOW - steps, 1)[:, :WINDOW]


def _sample_attn(q16, knt, vnt, kc, vc, rowinfo):
    nseq_total, nq, _ = q16.shape
    sb = SAMPLE_SEQS
    blk3 = lambda i: (i, 0, 0)
    return pl.pallas_call(
        _sample_attn_kernel,
        grid=(nseq_total // sb,),
        in_specs=[
            pl.BlockSpec((sb, nq, KV_W), blk3),
            pl.BlockSpec((None, KV_W, LANES), blk3),
            pl.BlockSpec((None, KV_W, LANES), blk3),
            pl.BlockSpec((sb, KV_W, WINDOW), blk3),
            pl.BlockSpec((sb, KV_W, WINDOW), blk3),
            pl.BlockSpec(rowinfo.shape, lambda i: (0, 0)),
        ],
        out_specs=[
            pl.BlockSpec((sb, nq, KV_W), blk3),
            pl.BlockSpec((sb, KV_W, WINDOW), blk3),
            pl.BlockSpec((sb, KV_W, WINDOW), blk3),
        ],
        out_shape=[
            jax.ShapeDtypeStruct((nseq_total, nq, KV_W), F32),
            jax.ShapeDtypeStruct((nseq_total, KV_W, WINDOW), F32),
            jax.ShapeDtypeStruct((nseq_total, KV_W, WINDOW), F32),
        ],
        compiler_params=pltpu.CompilerParams(dimension_semantics=("arbitrary",), vmem_limit_bytes=VMEM_LIMIT),
        name="sample_attn",
    )(q16, knt, vnt, kc, vc, rowinfo)


def _sample_tail_kernel(o_ref, gate_ref, h1_ref, p_ref, vec_ref, wout_ref, wpg_ref, wpp_ref, y_ref):
    o = jnp.concatenate([o_ref[hh] for hh in range(GROUP)], axis=1)
    h1 = h1_ref[...]
    h2 = h1 + _dot(o * _silu(gate_ref[...]), wout_ref[...])
    h2 = _ple(h2, p_ref[...], vec_ref[V1_PLE:V1_PLE + 1], wpg_ref, wpp_ref)
    y_ref[...] = _rms(h2, vec_ref[V1_FINAL:V1_FINAL + 1])


def _sample_tail(o4, gate, h1, p1_tm, vec1, w_out_b, wpg1, wpp1):
    args = (o4, gate, h1, p1_tm, vec1, w_out_b, wpg1, wpp1)
    return pl.pallas_call(
        _sample_tail_kernel,
        grid=(1,),
        in_specs=[_const_spec(a.shape) for a in args],
        out_specs=pl.BlockSpec(h1.shape, lambda i: (0, 0)),
        out_shape=jax.ShapeDtypeStruct(h1.shape, F32),
        compiler_params=pltpu.CompilerParams(dimension_semantics=("arbitrary",), vmem_limit_bytes=VMEM_LIMIT),
        name="sample_tail",
    )(*args)


def _heads_to_slot_major(w, axis):
    shape = w.shape
    w = w.reshape(shape[:axis] + (N_KV, GROUP, HEAD_DIM) + shape[axis + 1:])
    w = jnp.swapaxes(w, axis, axis + 1)
    return w.reshape(shape)


def _pad_rows(v, rows):
    return jnp.concatenate([v, jnp.zeros((rows - v.shape[0], v.shape[1]), v.dtype)], axis=0)


def kernel(x_prompt, x_sample, p_prompt, p_sample, state_lru_h, state_conv, cache_k, cache_v, norm_g, final_norm_g, ple_norm_g, w_ple_gate, w_ple_proj, w_in_a, conv_w_a, conv_b_a, w_rgate, b_rgate, w_igate, b_igate, lru_lambda, w_out_a, w_in_b, sinks, w_out_b):
    assert norm_g.shape[0] == 2 and w_in_a.shape[0] == 1 and w_in_b.shape[0] == 1
    nseq, steps, _ = x_sample.shape

    vec0 = _pad_rows(jnp.concatenate([norm_g[0:1], ple_norm_g[0:1], conv_b_a, b_rgate, b_igate, lru_lambda,
                                      conv_w_a[0]], axis=0), 2 * SUBLANES)
    vec1 = _pad_rows(jnp.concatenate([norm_g[1:2], ple_norm_g[1:2], final_norm_g[None]], axis=0), SUBLANES)
    w_in_a_b = w_in_a[0].astype(BF16)
    w_out_a_b = w_out_a[0].astype(BF16)
    wg = jnp.concatenate([w_rgate[0], w_igate[0]], axis=-1).astype(BF16)
    wpg = w_ple_gate.astype(BF16)
    wpp = w_ple_proj.astype(BF16)
    wb = w_in_b[0]
    w_in_b_b = jnp.concatenate([_heads_to_slot_major(wb[:, :Q_W], 1), wb[:, Q_W:Q_W + 2 * KV_W],
                                _heads_to_slot_major(wb[:, Q_W + 2 * KV_W:], 1)], axis=1).astype(BF16)
    w_out_b_b = _heads_to_slot_major(w_out_b[0], 0).astype(BF16)
    sinks_p = sinks[0].reshape(N_KV, GROUP).T.reshape(1, N_HEADS)

    h1_p, hl_p, tail_p = _prompt_l0(x_prompt, p_prompt, vec0, w_in_a_b, wg, w_out_a_b, wpg[0], wpp[0])
    y_prompt, kl_p, vl_p = _prompt_l1(sinks_p, h1_p, p_prompt, vec1, w_in_b_b, w_out_b_b, wpg[1], wpp[1])
    nbp = x_prompt.shape[0]
    new_h_p = hl_p[:, SUBLANES - 1][None]
    new_conv_p = tail_p[:, SUBLANES - (CONV_W - 1):][None]
    new_k_p = kl_p.reshape(1, nbp, WINDOW, N_KV, HEAD_DIM)
    new_v_p = vl_p.reshape(1, nbp, WINDOW, N_KV, HEAD_DIM)

    rows = steps * nseq
    x_tm = jnp.swapaxes(x_sample, 0, 1).reshape(rows, D_MODEL)
    p_tm = jnp.swapaxes(p_sample, 1, 2).reshape(2, rows, D_PLE)
    cst_tm = jnp.swapaxes(state_conv[0], 0, 1)
    h1_s, hl_s, cnew_tm, q_s, k_s, v_s, gate_s = _sample_l0(
        x_tm, p_tm[0], state_lru_h[0], cst_tm, vec0, vec1, w_in_a_b, wg, w_out_a_b, wpg[0], wpp[0], w_in_b_b)

    sb = SAMPLE_SEQS
    q16 = q_s.reshape(steps, nseq, GROUP, KV_W).transpose(1, 2, 0, 3).reshape(nseq, GROUP * steps, KV_W)

    def new_rows_t(a):
        a = a.reshape(steps, nseq // sb, sb, KV_W).transpose(1, 3, 2, 0).reshape(nseq // sb, KV_W, sb * steps)
        return jnp.concatenate([a, jnp.zeros((nseq // sb, KV_W, LANES - sb * steps), a.dtype)], axis=-1)

    kc = cache_k[0].transpose(0, 2, 3, 1).reshape(nseq, KV_W, WINDOW)
    vc = cache_v[0].transpose(0, 2, 3, 1).reshape(nseq, KV_W, WINDOW)
    head = jnp.arange(N_KV * GROUP * steps) // steps
    slopes = jnp.exp2(-8.0 * (head + 1).astype(F32) / N_HEADS)
    tq = (jnp.arange(N_KV * GROUP * steps) % steps).astype(F32)
    rowinfo = jnp.stack([sinks[0][head], slopes, tq], axis=1)
    rowinfo = jnp.concatenate([rowinfo, jnp.zeros((rowinfo.shape[0], LANES - 3), F32)], axis=1)
    o16, nk, nv = _sample_attn(q16, new_rows_t(k_s), new_rows_t(v_s), kc, vc, rowinfo)

    o4 = o16.reshape(nseq, GROUP, steps, KV_W).transpose(1, 2, 0, 3).reshape(GROUP, rows, KV_W)
    y_tm = _sample_tail(o4, gate_s, h1_s, p_tm[1], vec1, w_out_b_b, wpg[1], wpp[1])
    y_sample = jnp.swapaxes(y_tm.reshape(steps, nseq, D_MODEL), 0, 1)
    new_conv_s = jnp.swapaxes(cnew_tm, 0, 1)[None]
    new_k_s = nk.reshape(nseq, N_KV, HEAD_DIM, WINDOW).transpose(0, 3, 1, 2)[None]
    new_v_s = nv.reshape(nseq, N_KV, HEAD_DIM, WINDOW).transpose(0, 3, 1, 2)[None]

    return (y_prompt, y_sample, new_h_p, new_conv_p, new_k_p, new_v_p,
            hl_s[None], new_conv_s, new_k_s, new_v_s)
```

```python
import functools

import jax
import jax.numpy as jnp
from jax import lax
from jax.experimental import pallas as pl
from jax.experimental.pallas import tpu as pltpu

D_MODEL = 1024
D_RNN = 1024
D_PLE = 256
N_LRU_BLOCKS = 8
LRU_BLOCK = 128
CONV_W = 4
LRU_C = 8.0
EPS = 1e-6
HEAD_DIM = 64
N_HEADS = 16
N_KV = 4
GROUP = 4
Q_W = 1024
KV_W = 256
WINDOW = 128
BLOCK = 128
NEG_INF = -1e30
QK_SCALE = HEAD_DIM ** -0.5
SQRT_FLOOR = 1e-37

LANES = 128
SUBLANES = 8
VMEM_LIMIT = 56 * 1024 * 1024

PROMPT_TILE = 256
SAMPLE_SEQS = 8

F32 = jnp.float32
BF16 = jnp.bfloat16


def _rms(x, g):
    ms = jnp.mean(x * x, axis=-1, keepdims=True)
    return x * lax.rsqrt(ms + EPS) * g


def _sigmoid(x):
    return 0.5 * jnp.tanh(0.5 * x) + 0.5


def _silu(x):
    return x * _sigmoid(x)


def _dot(a, w):
    return jnp.dot(a.astype(BF16), w, preferred_element_type=F32)


def _softplus(z):
    return jnp.maximum(z, 0.0) + jnp.log1p(jnp.exp(-jnp.abs(z)))


def _lru_gate_logits(xc, wg_ref):
    xcb = xc.astype(BF16)
    zr, zi = [], []
    for n in range(N_LRU_BLOCKS):
        z = jnp.dot(xcb[:, n * LRU_BLOCK:(n + 1) * LRU_BLOCK], wg_ref[n], preferred_element_type=F32)
        zr.append(z[:, :LRU_BLOCK])
        zi.append(z[:, LRU_BLOCK:])
    return jnp.concatenate(zr, axis=1), jnp.concatenate(zi, axis=1)


def _lru_coeffs(xc, logits, b_r, b_i, lam):
    r = _sigmoid(logits[0] + b_r)
    i = _sigmoid(logits[1] + b_i)
    log_a = (-LRU_C * _softplus(-lam)) * r
    a = jnp.exp(log_a)
    y = -jnp.tanh(log_a) * (a * a + 1.0)
    b = (y * lax.rsqrt(jnp.maximum(y, SQRT_FLOOR))) * (i * xc)
    return a, b


def _ple(h, p, g, wpg_ref, wpp_ref):
    gate = _sigmoid(_dot(_rms(h, g), wpg_ref[...]))
    return h + gate * _dot(p, wpp_ref[...])


def _scan_rows(a, b, h0):
    t_rows, width = a.shape
    row = lax.broadcasted_iota(jnp.int32, (SUBLANES, width), 0)
    masks = [(s, row >= s) for s in (1, 2, 4)]
    carry = jnp.broadcast_to(h0, (SUBLANES, width))
    out = []
    for g in range(t_rows // SUBLANES):
        ag = a[g * SUBLANES:(g + 1) * SUBLANES]
        bg = b[g * SUBLANES:(g + 1) * SUBLANES]
        for s, m in masks:
            a_prev = pltpu.roll(ag, s, 0)
            b_prev = pltpu.roll(bg, s, 0)
            bg = bg + jnp.where(m, ag, 0.0) * b_prev
            ag = jnp.where(m, ag * a_prev, ag)
        hg = bg + ag * carry
        carry = jnp.broadcast_to(hg[SUBLANES - 1:SUBLANES], (SUBLANES, width))
        out.append(hg)
    return jnp.concatenate(out, axis=0)


def _shift_rows(x, prev8, k):
    row = lax.broadcasted_iota(jnp.int32, (SUBLANES, x.shape[1]), 0)
    first = row < k
    pieces = []
    prev = pltpu.roll(prev8, k, 0)
    for g in range(x.shape[0] // SUBLANES):
        cur = pltpu.roll(x[g * SUBLANES:(g + 1) * SUBLANES], k, 0)
        pieces.append(jnp.where(first, prev, cur))
        prev = cur
    return jnp.concatenate(pieces, axis=0)


V0_NORM, V0_PLE, V0_CB, V0_BR, V0_BI, V0_LAM, V0_CW = 0, 1, 2, 3, 4, 5, 6
V1_NORM, V1_PLE, V1_FINAL = 0, 1, 2


def _stage_weights(srcs, dsts):
    for src, dst in zip(srcs, dsts):
        dst[...] = src[...]


def _prompt_l0_kernel(x_ref, p_ref, vec_ref, win_in, wg_in, wout_in, wpg_in, wpp_in,
                      h1_ref, hl_ref, tail_ref, prev8, hcar, win_ref, wg_ref, wout_ref, wpg_ref, wpp_ref):
    t_rows = x_ref.shape[1]

    @pl.when(pl.program_id(0) == 0)
    def _():
        prev8[...] = jnp.zeros_like(prev8)
        hcar[...] = jnp.zeros_like(hcar)

    win_ref, wg_ref, wout_ref, wpg_ref, wpp_ref = win_in, wg_in, wout_in, wpg_in, wpp_in
    st = [dict() for _ in range(x_ref.shape[0])]

    def in_proj(n):
        st[n]["xg"] = _dot(_rms(x_ref[n], vec_ref[V0_NORM:V0_NORM + 1]), win_ref[...])
        st[n]["pp"] = _dot(p_ref[n], wpp_ref[...])

    def conv_gates(n):
        xb = st[n]["xg"][:, :D_RNN]
        p8 = prev8[n]
        acc = vec_ref[V0_CB:V0_CB + 1] + vec_ref[V0_CW + 3:V0_CW + 4] * xb
        for k in range(1, CONV_W):
            acc = acc + vec_ref[V0_CW + 3 - k:V0_CW + 4 - k] * _shift_rows(xb, p8, k)
        prev8[n] = xb[t_rows - SUBLANES:]
        tail_ref[n] = xb[t_rows - SUBLANES:]
        st[n]["xc"] = acc
        st[n]["logits"] = _lru_gate_logits(acc, wg_ref)

    def recurrence(n):
        a, b = _lru_coeffs(st[n]["xc"], st[n]["logits"], vec_ref[V0_BR:V0_BR + 1], vec_ref[V0_BI:V0_BI + 1],
                           vec_ref[V0_LAM:V0_LAM + 1])
        h = _scan_rows(a, b, hcar[n])
        hcar[n] = h[t_rows - 1:]
        hl_ref[n] = h[t_rows - SUBLANES:]
        st[n]["hg"] = h * _silu(st[n]["xg"][:, D_RNN:])

    def out_proj(n):
        h1 = x_ref[n] + _dot(st[n]["hg"], wout_ref[...])
        gl = _dot(_rms(h1, vec_ref[V0_PLE:V0_PLE + 1]), wpg_ref[...])
        h1_ref[n] = h1 + _sigmoid(gl) * st[n]["pp"]

    for stage in (in_proj, conv_gates, recurrence, out_proj):
        for n in range(len(st)):
            stage(n)


def _const_spec(shape):
    return pl.BlockSpec(shape, lambda *_: (0,) * len(shape), pipeline_mode=pl.Buffered(1))


def _prompt_l0(x, p_all, vec0, w_in_a, wg, w_out_a, wpg0, wpp0):
    nb, seq, _ = x.shape
    t = PROMPT_TILE
    return pl.pallas_call(
        _prompt_l0_kernel,
        grid=(seq // t,),
        in_specs=[
            pl.BlockSpec((nb, t, D_MODEL), lambda s: (0, s, 0)),
            pl.BlockSpec((None, nb, t, D_PLE), lambda s: (0, 0, s, 0)),
            _const_spec(vec0.shape), _const_spec(w_in_a.shape), _const_spec(wg.shape),
            _const_spec(w_out_a.shape), _const_spec(wpg0.shape), _const_spec(wpp0.shape),
        ],
        out_specs=[
            pl.BlockSpec((nb, t, D_MODEL), lambda s: (0, s, 0)),
            pl.BlockSpec((nb, SUBLANES, D_RNN), lambda s: (0, 0, 0)),
            pl.BlockSpec((nb, SUBLANES, D_RNN), lambda s: (0, 0, 0)),
        ],
        out_shape=[
            jax.ShapeDtypeStruct((nb, seq, D_MODEL), F32),
            jax.ShapeDtypeStruct((nb, SUBLANES, D_RNN), F32),
            jax.ShapeDtypeStruct((nb, SUBLANES, D_RNN), F32),
        ],
        scratch_shapes=[pltpu.VMEM((nb, SUBLANES, D_RNN), F32), pltpu.VMEM((nb, 1, D_RNN), F32)]
        + [pltpu.VMEM(w.shape, w.dtype) for w in (w_in_a, wg, w_out_a, wpg0, wpp0)],
        compiler_params=pltpu.CompilerParams(
            dimension_semantics=("arbitrary",), vmem_limit_bytes=VMEM_LIMIT),
        name="prompt_l0",
    )(x, p_all, vec0, w_in_a, wg, w_out_a, wpg0, wpp0)


def _slope(head):
    return 2.0 ** (-8.0 * (head + 1) / N_HEADS)


def _build_prompt_bias(bias_ref):
    shape = (BLOCK, 4 * BLOCK)
    qi = lax.broadcasted_iota(jnp.int32, shape, 0)
    col = lax.broadcasted_iota(jnp.int32, shape, 1)
    si = col & (2 * BLOCK - 1)
    second = col >= 2 * BLOCK
    dist = BLOCK + qi - si
    valid = (dist >= 0) & (dist < WINDOW)
    valid_first = valid & (si >= BLOCK)
    distf = dist.astype(F32)
    for hh in range(GROUP):
        for i in range(N_KV // 2):
            slope = jnp.where(second, _slope(GROUP * (2 * i + 1) + hh), _slope(GROUP * (2 * i) + hh))
            b = -(slope * distf)
            bias_ref[0, hh * 2 + i] = jnp.where(valid_first, b, NEG_INF)
            bias_ref[1, hh * 2 + i] = jnp.where(valid, b, NEG_INF)


def _masked_pair(x2, lo):
    return jnp.concatenate([jnp.where(lo, x2, 0.0), jnp.where(lo, 0.0, x2)], axis=0).astype(BF16)


def _attn_scores(q, k_prev, k_cur, bias_ref, tbl, lo):
    scores = []
    for i in range(N_KV // 2):
        lanes = slice(i * LANES, (i + 1) * LANES)
        k_both = _masked_pair(jnp.concatenate([k_prev[:, lanes], k_cur[:, lanes]], axis=0), lo)
        for hh in range(GROUP):
            q2 = q[:, hh * KV_W + i * LANES:hh * KV_W + (i + 1) * LANES]
            s2 = lax.dot_general(q2, k_both, (((1,), (1,)), ((), ())), preferred_element_type=F32)
            scores.append(s2 + bias_ref[tbl, hh * 2 + i])
    return scores


def _attn_values(scores, v_prev, v_cur, sink_ref, lo):
    cols = [None] * (2 * GROUP)
    for i in range(N_KV // 2):
        lanes = slice(i * LANES, (i + 1) * LANES)
        v_both = _masked_pair(jnp.concatenate([v_prev[:, lanes], v_cur[:, lanes]], axis=0), lo)
        for hh in range(GROUP):
            s2 = scores[i * GROUP + hh]
            es, inv = [], []
            for half in range(2):
                sink = sink_ref[0, hh * N_KV + 2 * i + half]
                sh = s2[:, half * 2 * BLOCK:(half + 1) * 2 * BLOCK]
                m = jnp.maximum(jnp.max(sh, axis=-1, keepdims=True), sink)
                e = jnp.exp(sh - m)
                denom = jnp.sum(e, axis=-1, keepdims=True) + jnp.exp(sink - m)
                es.append(e.astype(BF16))
                inv.append(1.0 / denom)
            o2 = jnp.dot(jnp.concatenate(es, axis=1), v_both, preferred_element_type=F32)
            cols[hh * 2 + i] = o2 * jnp.where(lo, inv[0], inv[1])
    return jnp.concatenate(cols, axis=1)


def _prompt_l1_kernel(sink_ref, h1_ref, p_ref, vec_ref, win_in, wout_in, wpg_in, wpp_in,
                      y_ref, kl_ref, vl_ref, kprev, vprev, bias_ref, win_ref, wout_ref, wpg_ref, wpp_ref):
    nstreams, t_rows = h1_ref.shape[0], h1_ref.shape[1]
    nblk = t_rows // BLOCK
    s_idx = pl.program_id(0)

    @pl.when(s_idx == 0)
    def _():
        _build_prompt_bias(bias_ref)
        kprev[...] = jnp.zeros_like(kprev)
        vprev[...] = jnp.zeros_like(vprev)
        _stage_weights((win_in, wout_in, wpg_in, wpp_in), (win_ref, wout_ref, wpg_ref, wpp_ref))

    lo = lax.broadcasted_iota(jnp.int32, (1, LANES), 1) < HEAD_DIM
    first_tbl = jnp.where(s_idx == 0, 0, 1)
    streams = range(nstreams)
    proj = [_dot(_rms(h1_ref[n], vec_ref[V1_NORM:V1_NORM + 1]), win_ref[...]) for n in streams]
    pp = [_dot(p_ref[n], wpp_ref[...]) for n in streams]

    def kv_block(n, j, base, carry_ref):
        if j < 0:
            return carry_ref[n]
        return proj[n][j * BLOCK:(j + 1) * BLOCK, base:base + KV_W]

    def scores_of(n, j):
        q = (proj[n][j * BLOCK:(j + 1) * BLOCK, :Q_W] * QK_SCALE).astype(BF16)
        return _attn_scores(q, kv_block(n, j - 1, Q_W, kprev), kv_block(n, j, Q_W, kprev), bias_ref,
                            first_tbl if j == 0 else 1, lo)

    def finish(n, o_blocks):
        k_last = kv_block(n, nblk - 1, Q_W, kprev)
        v_last = kv_block(n, nblk - 1, Q_W + KV_W, vprev)
        kprev[n] = k_last
        vprev[n] = v_last
        kl_ref[n] = k_last
        vl_ref[n] = v_last
        gate = proj[n][:, Q_W + 2 * KV_W:]
        h2 = h1_ref[n] + _dot(jnp.concatenate(o_blocks, axis=0) * _silu(gate), wout_ref[...])
        gl = _dot(_rms(h2, vec_ref[V1_PLE:V1_PLE + 1]), wpg_ref[...])
        h2 = h2 + _sigmoid(gl) * pp[n]
        y_ref[n] = _rms(h2, vec_ref[V1_FINAL:V1_FINAL + 1])

    order = [(n, j) for n in streams for j in range(nblk)]
    o_blocks = {n: [] for n in streams}
    pending = None
    for unit in order + [None]:
        scores = scores_of(*unit) if unit is not None else None
        if pending is not None:
            (n, j), sc = pending
            o_blocks[n].append(_attn_values(sc, kv_block(n, j - 1, Q_W + KV_W, vprev),
                                            kv_block(n, j, Q_W + KV_W, vprev), sink_ref, lo))
            if j == nblk - 1:
                finish(n, o_blocks[n])
        pending = (unit, scores)


def _prompt_l1(sinks_p, h1, p_all, vec1, w_in_b, w_out_b, wpg1, wpp1):
    nb, seq, _ = h1.shape
    t = PROMPT_TILE
    return pl.pallas_call(
        _prompt_l1_kernel,
        grid=(seq // t,),
        in_specs=[
            pl.BlockSpec(memory_space=pltpu.SMEM),
            pl.BlockSpec((nb, t, D_MODEL), lambda s: (0, s, 0)),
            pl.BlockSpec((None, nb, t, D_PLE), lambda s: (1, 0, s, 0)),
            _const_spec(vec1.shape), _const_spec(w_in_b.shape), _const_spec(w_out_b.shape),
            _const_spec(wpg1.shape), _const_spec(wpp1.shape),
        ],
        out_specs=[
            pl.BlockSpec((nb, t, D_MODEL), lambda s: (0, s, 0)),
            pl.BlockSpec((nb, BLOCK, KV_W), lambda s: (0, 0, 0)),
            pl.BlockSpec((nb, BLOCK, KV_W), lambda s: (0, 0, 0)),
        ],
        out_shape=[
            jax.ShapeDtypeStruct((nb, seq, D_MODEL), F32),
            jax.ShapeDtypeStruct((nb, BLOCK, KV_W), F32),
            jax.ShapeDtypeStruct((nb, BLOCK, KV_W), F32),
        ],
        scratch_shapes=[
            pltpu.VMEM((nb, BLOCK, KV_W), F32), pltpu.VMEM((nb, BLOCK, KV_W), F32),
            pltpu.VMEM((2, 2 * GROUP, BLOCK, 4 * BLOCK), F32),
        ] + [pltpu.VMEM(w.shape, w.dtype) for w in (w_in_b, w_out_b, wpg1, wpp1)],
        compiler_params=pltpu.CompilerParams(
            dimension_semantics=("arbitrary",), vmem_limit_bytes=VMEM_LIMIT),
        name="prompt_l1",
    )(sinks_p, h1, p_all, vec1, w_in_b, w_out_b, wpg1, wpp1)


def _sample_l0_kernel(x_ref, p_ref, h0_ref, cst_ref, vec_ref, vec1_ref, win_ref, wg_ref, wout_ref, wpg_ref, wpp_ref,
                      winb_ref, h1_ref, hl_ref, cnew_ref, q_ref, k_ref, v_ref, gate_ref):
    nseq = h0_ref.shape[0]
    steps = x_ref.shape[0] // nseq
    x = x_ref[...]
    xg = _dot(_rms(x, vec_ref[V0_NORM:V0_NORM + 1]), win_ref[...])
    xb = xg[:, :D_RNN]
    gate = xg[:, D_RNN:]
    hist = [cst_ref[k] for k in range(CONV_W - 1)] + [xb[t * nseq:(t + 1) * nseq] for t in range(steps)]
    xc = []
    for t in range(steps):
        acc = vec_ref[V0_CB:V0_CB + 1] + vec_ref[V0_CW:V0_CW + 1] * hist[t]
        for k in range(1, CONV_W):
            acc = acc + vec_ref[V0_CW + k:V0_CW + k + 1] * hist[t + k]
        xc.append(acc)
    for k in range(CONV_W - 1):
        cnew_ref[k] = hist[steps + k]
    xc = jnp.concatenate(xc, axis=0)
    a, b = _lru_coeffs(xc, _lru_gate_logits(xc, wg_ref), vec_ref[V0_BR:V0_BR + 1], vec_ref[V0_BI:V0_BI + 1],
                       vec_ref[V0_LAM:V0_LAM + 1])
    h = h0_ref[...]
    hs = []
    for t in range(steps):
        h = a[t * nseq:(t + 1) * nseq] * h + b[t * nseq:(t + 1) * nseq]
        hs.append(h)
    hl_ref[...] = h
    h1 = x + _dot(jnp.concatenate(hs, axis=0) * _silu(gate), wout_ref[...])
    h1 = _ple(h1, p_ref[...], vec_ref[V0_PLE:V0_PLE + 1], wpg_ref, wpp_ref)
    h1_ref[...] = h1
    proj = _dot(_rms(h1, vec1_ref[V1_NORM:V1_NORM + 1]), winb_ref[...])
    q_ref[...] = proj[:, :Q_W] * QK_SCALE
    k_ref[...] = proj[:, Q_W:Q_W + KV_W]
    v_ref[...] = proj[:, Q_W + KV_W:Q_W + 2 * KV_W]
    gate_ref[...] = proj[:, Q_W + 2 * KV_W:]


def _sample_l0(x_tm, p0_tm, h0, cst_tm, vec0, vec1, w_in_a, wg, w_out_a, wpg0, wpp0, w_in_b):
    rows = x_tm.shape[0]
    nseq = h0.shape[0]
    args = (x_tm, p0_tm, h0, cst_tm, vec0, vec1, w_in_a, wg, w_out_a, wpg0, wpp0, w_in_b)
    out_shapes = [(rows, D_MODEL), (nseq, D_RNN), (CONV_W - 1, nseq, D_RNN),
                  (rows, Q_W), (rows, KV_W), (rows, KV_W), (rows, Q_W)]
    return pl.pallas_call(
        _sample_l0_kernel,
        grid=(1,),
        in_specs=[_const_spec(a.shape) for a in args],
        out_specs=[pl.BlockSpec(s, lambda *_, n=len(s): (0,) * n) for s in out_shapes],
        out_shape=[jax.ShapeDtypeStruct(s, F32) for s in out_shapes],
        compiler_params=pltpu.CompilerParams(dimension_semantics=("arbitrary",), vmem_limit_bytes=VMEM_LIMIT),
        name="sample_l0",
    )(*args)


def _sample_attn_kernel(q_ref, knt_ref, vnt_ref, kc_ref, vc_ref, rowinfo_ref, o_ref, nk_ref, nv_ref):
    nseq = q_ref.shape[0]
    nq = q_ref.shape[1]
    steps = nq // GROUP
    rows = N_KV * nq
    sink = rowinfo_ref[:, 0:1]
    slope = rowinfo_ref[:, 1:2]
    tq = rowinfo_ref[:, 2:3]
    col = lax.broadcasted_iota(jnp.int32, (rows, 2 * WINDOW), 1).astype(F32)
    dist = WINDOW + tq - col
    valid = (dist >= 0.0) & (dist < WINDOW)
    bias = jnp.where(valid, -(slope * dist), NEG_INF)
    lane_grp = lax.broadcasted_iota(jnp.int32, (1, KV_W), 1) // HEAD_DIM
    new_lane = lax.broadcasted_iota(jnp.int32, (1, WINDOW), 1) < steps
    knt_all = knt_ref[...]
    vnt_all = vnt_ref[...]

    for bl in range(nseq):
        q = q_ref[bl]
        q_full = jnp.concatenate([jnp.where(lane_grp == g, q, 0.0) for g in range(N_KV)], axis=0).astype(BF16)
        shift = (LANES - steps * bl) % LANES
        knt = knt_all if shift == 0 else pltpu.roll(knt_all, shift, 1)
        vnt = vnt_all if shift == 0 else pltpu.roll(vnt_all, shift, 1)
        kt_ext = jnp.concatenate([kc_ref[bl], jnp.where(new_lane, knt, 0.0)], axis=1)
        vt_ext = jnp.concatenate([vc_ref[bl], jnp.where(new_lane, vnt, 0.0)], axis=1)
        s = jnp.dot(q_full, kt_ext.astype(BF16), preferred_element_type=F32) + bias
        m = jnp.maximum(jnp.max(s, axis=-1, keepdims=True), sink)
        e = jnp.exp(s - m)
        denom = jnp.sum(e, axis=-1, keepdims=True) + jnp.exp(sink - m)
        o = lax.dot_general(e.astype(BF16), vt_ext.astype(BF16), (((1,), (1,)), ((), ())),
                            preferred_element_type=F32) * (1.0 / denom)
        acc = jnp.where(lane_grp == 0, o[:nq], 0.0)
        for g in range(1, N_KV):
            acc = acc + jnp.where(lane_grp == g, o[g * nq:(g + 1) * nq], 0.0)
        o_ref[bl] = acc
        nk_ref[bl] = pltpu.roll(kt_ext, 2 * WINDOW - steps, 1)[:, :WINDOW]
        nv_ref[bl] = pltpu.roll(vt_ext, 2 * WINDOW - steps, 1)[:, :WINDOW]


def _sample_attn(q16, knt, vnt, kc, vc, rowinfo):
    nseq_total, nq, _ = q16.shape
    sb = SAMPLE_SEQS
    blk3 = lambda i: (i, 0, 0)
    return pl.pallas_call(
        _sample_attn_kernel,
        grid=(nseq_total // sb,),
        in_specs=[
            pl.BlockSpec((sb, nq, KV_W), blk3),
            pl.BlockSpec((None, KV_W, LANES), blk3),
            pl.BlockSpec((None, KV_W, LANES), blk3),
            pl.BlockSpec((sb, KV_W, WINDOW), blk3),
            pl.BlockSpec((sb, KV_W, WINDOW), blk3),
            pl.BlockSpec(rowinfo.shape, lambda i: (0, 0)),
        ],
        out_specs=[
            pl.BlockSpec((sb, nq, KV_W), blk3),
            pl.BlockSpec((sb, KV_W, WINDOW), blk3),
            pl.BlockSpec((sb, KV_W, WINDOW), blk3),
        ],
        out_shape=[
            jax.ShapeDtypeStruct((nseq_total, nq, KV_W), F32),
            jax.ShapeDtypeStruct((nseq_total, KV_W, WINDOW), F32),
            jax.ShapeDtypeStruct((nseq_total, KV_W, WINDOW), F32),
        ],
        compiler_params=pltpu.CompilerParams(dimension_semantics=("arbitrary",), vmem_limit_bytes=VMEM_LIMIT),
        name="sample_attn",
    )(q16, knt, vnt, kc, vc, rowinfo)


def _sample_tail_kernel(o_ref, gate_ref, h1_ref, p_ref, vec_ref, wout_ref, wpg_ref, wpp_ref, y_ref):
    o = jnp.concatenate([o_ref[hh] for hh in range(GROUP)], axis=1)
    h1 = h1_ref[...]
    h2 = h1 + _dot(o * _silu(gate_ref[...]), wout_ref[...])
    h2 = _ple(h2, p_ref[...], vec_ref[V1_PLE:V1_PLE + 1], wpg_ref, wpp_ref)
    y_ref[...] = _rms(h2, vec_ref[V1_FINAL:V1_FINAL + 1])


def _sample_tail(o4, gate, h1, p1_tm, vec1, w_out_b, wpg1, wpp1):
    args = (o4, gate, h1, p1_tm, vec1, w_out_b, wpg1, wpp1)
    return pl.pallas_call(
        _sample_tail_kernel,
        grid=(1,),
        in_specs=[_const_spec(a.shape) for a in args],
        out_specs=pl.BlockSpec(h1.shape, lambda i: (0, 0)),
        out_shape=jax.ShapeDtypeStruct(h1.shape, F32),
        compiler_params=pltpu.CompilerParams(dimension_semantics=("arbitrary",), vmem_limit_bytes=VMEM_LIMIT),
        name="sample_tail",
    )(*args)


def _heads_to_slot_major(w, axis):
    shape = w.shape
    w = w.reshape(shape[:axis] + (N_KV, GROUP, HEAD_DIM) + shape[axis + 1:])
    w = jnp.swapaxes(w, axis, axis + 1)
    return w.reshape(shape)


def _pad_rows(v, rows):
    return jnp.concatenate([v, jnp.zeros((rows - v.shape[0], v.shape[1]), v.dtype)], axis=0)


def kernel(x_prompt, x_sample, p_prompt, p_sample, state_lru_h, state_conv, cache_k, cache_v, norm_g, final_norm_g, ple_norm_g, w_ple_gate, w_ple_proj, w_in_a, conv_w_a, conv_b_a, w_rgate, b_rgate, w_igate, b_igate, lru_lambda, w_out_a, w_in_b, sinks, w_out_b):
    assert norm_g.shape[0] == 2 and w_in_a.shape[0] == 1 and w_in_b.shape[0] == 1
    nseq, steps, _ = x_sample.shape

    vec0 = _pad_rows(jnp.concatenate([norm_g[0:1], ple_norm_g[0:1], conv_b_a, b_rgate, b_igate, lru_lambda,
                                      conv_w_a[0]], axis=0), 2 * SUBLANES)
    vec1 = _pad_rows(jnp.concatenate([norm_g[1:2], ple_norm_g[1:2], final_norm_g[None]], axis=0), SUBLANES)
    w_in_a_b = w_in_a[0].astype(BF16)
    w_out_a_b = w_out_a[0].astype(BF16)
    wg = jnp.concatenate([w_rgate[0], w_igate[0]], axis=-1).astype(BF16)
    wpg = w_ple_gate.astype(BF16)
    wpp = w_ple_proj.astype(BF16)
    wb = w_in_b[0]
    w_in_b_b = jnp.concatenate([_heads_to_slot_major(wb[:, :Q_W], 1), wb[:, Q_W:Q_W + 2 * KV_W],
                                _heads_to_slot_major(wb[:, Q_W + 2 * KV_W:], 1)], axis=1).astype(BF16)
    w_out_b_b = _heads_to_slot_major(w_out_b[0], 0).astype(BF16)
    sinks_p = sinks[0].reshape(N_KV, GROUP).T.reshape(1, N_HEADS)

    h1_p, hl_p, tail_p = _prompt_l0(x_prompt, p_prompt, vec0, w_in_a_b, wg, w_out_a_b, wpg[0], wpp[0])
    y_prompt, kl_p, vl_p = _prompt_l1(sinks_p, h1_p, p_prompt, vec1, w_in_b_b, w_out_b_b, wpg[1], wpp[1])
    nbp = x_prompt.shape[0]
    new_h_p = hl_p[:, SUBLANES - 1][None]
    new_conv_p = tail_p[:, SUBLANES - (CONV_W - 1):][None]
    new_k_p = kl_p.reshape(1, nbp, WINDOW, N_KV, HEAD_DIM)
    new_v_p = vl_p.reshape(1, nbp, WINDOW, N_KV, HEAD_DIM)

    rows = steps * nseq
    x_tm = jnp.swapaxes(x_sample, 0, 1).reshape(rows, D_MODEL)
    p_tm = jnp.swapaxes(p_sample, 1, 2).reshape(2, rows, D_PLE)
    cst_tm = jnp.swapaxes(state_conv[0], 0, 1)
    h1_s, hl_s, cnew_tm, q_s, k_s, v_s, gate_s = _sample_l0(
        x_tm, p_tm[0], state_lru_h[0], cst_tm, vec0, vec1, w_in_a_b, wg, w_out_a_b, wpg[0], wpp[0], w_in_b_b)

    sb = SAMPLE_SEQS
    q16 = q_s.reshape(steps, nseq, GROUP, KV_W).transpose(1, 2, 0, 3).reshape(nseq, GROUP * steps, KV_W)

    def new_rows_t(a):
        a = a.reshape(steps, nseq // sb, sb, KV_W).transpose(1, 3, 2, 0).reshape(nseq // sb, KV_W, sb * steps)
        return jnp.concatenate([a, jnp.zeros((nseq // sb, KV_W, LANES - sb * steps), a.dtype)], axis=-1)

    kc = cache_k[0].transpose(0, 2, 3, 1).reshape(nseq, KV_W, WINDOW)
    vc = cache_v[0].transpose(0, 2, 3, 1).reshape(nseq, KV_W, WINDOW)
    head = jnp.arange(N_KV * GROUP * steps) // steps
    slopes = jnp.exp2(-8.0 * (head + 1).astype(F32) / N_HEADS)
    tq = (jnp.arange(N_KV * GROUP * steps) % steps).astype(F32)
    rowinfo = jnp.stack([sinks[0][head], slopes, tq], axis=1)
    rowinfo = jnp.concatenate([rowinfo, jnp.zeros((rowinfo.shape[0], LANES - 3), F32)], axis=1)
    o16, nk, nv = _sample_attn(q16, new_rows_t(k_s), new_rows_t(v_s), kc, vc, rowinfo)

    o4 = o16.reshape(nseq, GROUP, steps, KV_W).transpose(1, 2, 0, 3).reshape(GROUP, rows, KV_W)
    y_tm = _sample_tail(o4, gate_s, h1_s, p_tm[1], vec1, w_out_b_b, wpg[1], wpp[1])
    y_sample = jnp.swapaxes(y_tm.reshape(steps, nseq, D_MODEL), 0, 1)
    new_conv_s = jnp.swapaxes(cnew_tm, 0, 1)[None]
    new_k_s = nk.reshape(nseq, N_KV, HEAD_DIM, WINDOW).transpose(0, 3, 1, 2)[None]
    new_v_s = nv.reshape(nseq, N_KV, HEAD_DIM, WINDOW).transpose(0, 3, 1, 2)[None]

    return (y_prompt, y_sample, new_h_p, new_conv_p, new_k_p, new_v_p,
            hl_s[None], new_conv_s, new_k_s, new_v_s)
```

```python
import functools

import jax
import jax.numpy as jnp
from jax import lax
from jax.experimental import pallas as pl
from jax.experimental.pallas import tpu as pltpu

D_MODEL = 1024
D_RNN = 1024
D_PLE = 256
N_LRU_BLOCKS = 8
LRU_BLOCK = 128
CONV_W = 4
LRU_C = 8.0
EPS = 1e-6
HEAD_DIM = 64
N_HEADS = 16
N_KV = 4
GROUP = 4
Q_W = 1024
KV_W = 256
WINDOW = 128
BLOCK = 128
NEG_INF = -1e30
QK_SCALE = HEAD_DIM ** -0.5
SQRT_FLOOR = 1e-37

LANES = 128
SUBLANES = 8
VMEM_LIMIT = 56 * 1024 * 1024

PROMPT_TILE = 256
SAMPLE_SEQS = 8

F32 = jnp.float32
BF16 = jnp.bfloat16


def _rms(x, g):
    ms = jnp.mean(x * x, axis=-1, keepdims=True)
    return x * lax.rsqrt(ms + EPS) * g


def _sigmoid(x):
    return 0.5 * jnp.tanh(0.5 * x) + 0.5


def _silu(x):
    return x * _sigmoid(x)


def _dot(a, w):
    return jnp.dot(a.astype(BF16), w, preferred_element_type=F32)


def _softplus(z):
    return jnp.maximum(z, 0.0) + jnp.log1p(jnp.exp(-jnp.abs(z)))


def _lru_gate_logits(xc, wg_ref):
    xcb = xc.astype(BF16)
    zr, zi = [], []
    for n in range(N_LRU_BLOCKS):
        z = jnp.dot(xcb[:, n * LRU_BLOCK:(n + 1) * LRU_BLOCK], wg_ref[n], preferred_element_type=F32)
        zr.append(z[:, :LRU_BLOCK])
        zi.append(z[:, LRU_BLOCK:])
    return jnp.concatenate(zr, axis=1), jnp.concatenate(zi, axis=1)


def _lru_coeffs(xc, logits, b_r, b_i, lam):
    r = _sigmoid(logits[0] + b_r)
    i = _sigmoid(logits[1] + b_i)
    log_a = (-LRU_C * _softplus(-lam)) * r
    a = jnp.exp(log_a)
    y = -jnp.tanh(log_a) * (a * a + 1.0)
    b = (y * lax.rsqrt(jnp.maximum(y, SQRT_FLOOR))) * (i * xc)
    return a, b


def _ple(h, p, g, wpg_ref, wpp_ref):
    gate = _sigmoid(_dot(_rms(h, g), wpg_ref[...]))
    return h + gate * _dot(p, wpp_ref[...])


def _scan_rows(a, b, h0):
    t_rows, width = a.shape
    row = lax.broadcasted_iota(jnp.int32, (SUBLANES, width), 0)
    masks = [(s, row >= s) for s in (1, 2, 4)]
    carry = jnp.broadcast_to(h0, (SUBLANES, width))
    out = []
    for g in range(t_rows // SUBLANES):
        ag = a[g * SUBLANES:(g + 1) * SUBLANES]
        bg = b[g * SUBLANES:(g + 1) * SUBLANES]
        for s, m in masks:
            a_prev = pltpu.roll(ag, s, 0)
            b_prev = pltpu.roll(bg, s, 0)
            bg = bg + jnp.where(m, ag, 0.0) * b_prev
            ag = jnp.where(m, ag * a_prev, ag)
        hg = bg + ag * carry
        carry = jnp.broadcast_to(hg[SUBLANES - 1:SUBLANES], (SUBLANES, width))
        out.append(hg)
    return jnp.concatenate(out, axis=0)


def _shift_rows(x, prev8, k):
    row = lax.broadcasted_iota(jnp.int32, (SUBLANES, x.shape[1]), 0)
    first = row < k
    pieces = []
    prev = pltpu.roll(prev8, k, 0)
    for g in range(x.shape[0] // SUBLANES):
        cur = pltpu.roll(x[g * SUBLANES:(g + 1) * SUBLANES], k, 0)
        pieces.append(jnp.where(first, prev, cur))
        prev = cur
    return jnp.concatenate(pieces, axis=0)


V0_NORM, V0_PLE, V0_CB, V0_BR, V0_BI, V0_LAM, V0_CW = 0, 1, 2, 3, 4, 5, 6
V1_NORM, V1_PLE, V1_FINAL = 0, 1, 2


def _stage_weights(srcs, dsts):
    for src, dst in zip(srcs, dsts):
        dst[...] = src[...]


def _prompt_l0_kernel(x_ref, p_ref, vec_ref, win_in, wg_in, wout_in, wpg_in, wpp_in,
                      h1_ref, hl_ref, tail_ref, prev8, hcar, win_ref, wg_ref, wout_ref, wpg_ref, wpp_ref):
    t_rows = x_ref.shape[1]

    @pl.when(pl.program_id(0) == 0)
    def _():
        prev8[...] = jnp.zeros_like(prev8)
        hcar[...] = jnp.zeros_like(hcar)

    win_ref, wg_ref, wout_ref, wpg_ref, wpp_ref = win_in, wg_in, wout_in, wpg_in, wpp_in
    st = [dict() for _ in range(x_ref.shape[0])]

    def in_proj(n):
        st[n]["xg"] = _dot(_rms(x_ref[n], vec_ref[V0_NORM:V0_NORM + 1]), win_ref[...])
        st[n]["pp"] = _dot(p_ref[n], wpp_ref[...])

    def conv_gates(n):
        xb = st[n]["xg"][:, :D_RNN]
        p8 = prev8[n]
        acc = vec_ref[V0_CB:V0_CB + 1] + vec_ref[V0_CW + 3:V0_CW + 4] * xb
        for k in range(1, CONV_W):
            acc = acc + vec_ref[V0_CW + 3 - k:V0_CW + 4 - k] * _shift_rows(xb, p8, k)
        prev8[n] = xb[t_rows - SUBLANES:]
        tail_ref[n] = xb[t_rows - SUBLANES:]
        st[n]["xc"] = acc
        st[n]["logits"] = _lru_gate_logits(acc, wg_ref)

    def recurrence(n):
        a, b = _lru_coeffs(st[n]["xc"], st[n]["logits"], vec_ref[V0_BR:V0_BR + 1], vec_ref[V0_BI:V0_BI + 1],
                           vec_ref[V0_LAM:V0_LAM + 1])
        h = _scan_rows(a, b, hcar[n])
        hcar[n] = h[t_rows - 1:]
        hl_ref[n] = h[t_rows - SUBLANES:]
        st[n]["hg"] = h * _silu(st[n]["xg"][:, D_RNN:])

    def out_proj(n):
        st[n]["h1"] = x_ref[n] + _dot(st[n]["hg"], wout_ref[...])

    def ple(n):
        h1 = st[n]["h1"]
        gl = _dot(_rms(h1, vec_ref[V0_PLE:V0_PLE + 1]), wpg_ref[...])
        h1_ref[n] = h1 + _sigmoid(gl) * st[n]["pp"]

    for stage in (in_proj, conv_gates):
        for n in range(len(st)):
            stage(n)
    for n in range(len(st)):
        recurrence(n)
        out_proj(n)
    for n in range(len(st)):
        ple(n)


def _const_spec(shape):
    return pl.BlockSpec(shape, lambda *_: (0,) * len(shape), pipeline_mode=pl.Buffered(1))


def _prompt_l0(x, p_all, vec0, w_in_a, wg, w_out_a, wpg0, wpp0):
    nb, seq, _ = x.shape
    t = PROMPT_TILE
    return pl.pallas_call(
        _prompt_l0_kernel,
        grid=(seq // t,),
        in_specs=[
            pl.BlockSpec((nb, t, D_MODEL), lambda s: (0, s, 0)),
            pl.BlockSpec((None, nb, t, D_PLE), lambda s: (0, 0, s, 0)),
            _const_spec(vec0.shape), _const_spec(w_in_a.shape), _const_spec(wg.shape),
            _const_spec(w_out_a.shape), _const_spec(wpg0.shape), _const_spec(wpp0.shape),
        ],
        out_specs=[
            pl.BlockSpec((nb, t, D_MODEL), lambda s: (0, s, 0)),
            pl.BlockSpec((nb, SUBLANES, D_RNN), lambda s: (0, 0, 0)),
            pl.BlockSpec((nb, SUBLANES, D_RNN), lambda s: (0, 0, 0)),
        ],
        out_shape=[
            jax.ShapeDtypeStruct((nb, seq, D_MODEL), F32),
            jax.ShapeDtypeStruct((nb, SUBLANES, D_RNN), F32),
            jax.ShapeDtypeStruct((nb, SUBLANES, D_RNN), F32),
        ],
        scratch_shapes=[pltpu.VMEM((nb, SUBLANES, D_RNN), F32), pltpu.VMEM((nb, 1, D_RNN), F32)]
        + [pltpu.VMEM(w.shape, w.dtype) for w in (w_in_a, wg, w_out_a, wpg0, wpp0)],
        compiler_params=pltpu.CompilerParams(
            dimension_semantics=("arbitrary",), vmem_limit_bytes=VMEM_LIMIT),
        name="prompt_l0",
    )(x, p_all, vec0, w_in_a, wg, w_out_a, wpg0, wpp0)


def _slope(head):
    return 2.0 ** (-8.0 * (head + 1) / N_HEADS)


def _build_prompt_bias(bias_ref):
    shape = (BLOCK, 4 * BLOCK)
    qi = lax.broadcasted_iota(jnp.int32, shape, 0)
    col = lax.broadcasted_iota(jnp.int32, shape, 1)
    si = col & (2 * BLOCK - 1)
    second = col >= 2 * BLOCK
    dist = BLOCK + qi - si
    valid = (dist >= 0) & (dist < WINDOW)
    valid_first = valid & (si >= BLOCK)
    distf = dist.astype(F32)
    for hh in range(GROUP):
        for i in range(N_KV // 2):
            slope = jnp.where(second, _slope(GROUP * (2 * i + 1) + hh), _slope(GROUP * (2 * i) + hh))
            b = -(slope * distf)
            bias_ref[0, hh * 2 + i] = jnp.where(valid_first, b, NEG_INF)
            bias_ref[1, hh * 2 + i] = jnp.where(valid, b, NEG_INF)


def _masked_pair(x2, lo):
    return jnp.concatenate([jnp.where(lo, x2, 0.0), jnp.where(lo, 0.0, x2)], axis=0).astype(BF16)


def _attn_scores(q, k_prev, k_cur, bias_ref, tbl, lo):
    scores = []
    for i in range(N_KV // 2):
        lanes = slice(i * LANES, (i + 1) * LANES)
        k_both = _masked_pair(jnp.concatenate([k_prev[:, lanes], k_cur[:, lanes]], axis=0), lo)
        for hh in range(GROUP):
            q2 = q[:, hh * KV_W + i * LANES:hh * KV_W + (i + 1) * LANES]
            s2 = lax.dot_general(q2, k_both, (((1,), (1,)), ((), ())), preferred_element_type=F32)
            scores.append(s2 + bias_ref[tbl, hh * 2 + i])
    return scores


def _attn_values(scores, v_prev, v_cur, sink_ref, lo):
    cols = [None] * (2 * GROUP)
    for i in range(N_KV // 2):
        lanes = slice(i * LANES, (i + 1) * LANES)
        v_both = _masked_pair(jnp.concatenate([v_prev[:, lanes], v_cur[:, lanes]], axis=0), lo)
        for hh in range(GROUP):
            s2 = scores[i * GROUP + hh]
            es, inv = [], []
            for half in range(2):
                sink = sink_ref[0, hh * N_KV + 2 * i + half]
                sh = s2[:, half * 2 * BLOCK:(half + 1) * 2 * BLOCK]
                m = jnp.maximum(jnp.max(sh, axis=-1, keepdims=True), sink)
                e = jnp.exp(sh - m)
                denom = jnp.sum(e, axis=-1, keepdims=True) + jnp.exp(sink - m)
                es.append(e.astype(BF16))
                inv.append(1.0 / denom)
            o2 = jnp.dot(jnp.concatenate(es, axis=1), v_both, preferred_element_type=F32)
            cols[hh * 2 + i] = o2 * jnp.where(lo, inv[0], inv[1])
    return jnp.concatenate(cols, axis=1)


def _prompt_l1_kernel(sink_ref, h1_ref, p_ref, vec_ref, win_in, wout_in, wpg_in, wpp_in,
                      y_ref, kl_ref, vl_ref, kprev, vprev, bias_ref, win_ref, wout_ref, wpg_ref, wpp_ref):
    nstreams, t_rows = h1_ref.shape[0], h1_ref.shape[1]
    nblk = t_rows // BLOCK
    s_idx = pl.program_id(0)

    @pl.when(s_idx == 0)
    def _():
        _build_prompt_bias(bias_ref)
        kprev[...] = jnp.zeros_like(kprev)
        vprev[...] = jnp.zeros_like(vprev)
        _stage_weights((win_in, wout_in, wpg_in, wpp_in), (win_ref, wout_ref, wpg_ref, wpp_ref))

    lo = lax.broadcasted_iota(jnp.int32, (1, LANES), 1) < HEAD_DIM
    first_tbl = jnp.where(s_idx == 0, 0, 1)
    streams = range(nstreams)
    proj = [_dot(_rms(h1_ref[n], vec_ref[V1_NORM:V1_NORM + 1]), win_ref[...]) for n in streams]
    pp = [_dot(p_ref[n], wpp_ref[...]) for n in streams]

    def kv_block(n, j, base, carry_ref):
        if j < 0:
            return carry_ref[n]
        return proj[n][j * BLOCK:(j + 1) * BLOCK, base:base + KV_W]

    def scores_of(n, j):
        q = (proj[n][j * BLOCK:(j + 1) * BLOCK, :Q_W] * QK_SCALE).astype(BF16)
        return _attn_scores(q, kv_block(n, j - 1, Q_W, kprev), kv_block(n, j, Q_W, kprev), bias_ref,
                            first_tbl if j == 0 else 1, lo)

    def finish(n, o_blocks):
        k_last = kv_block(n, nblk - 1, Q_W, kprev)
        v_last = kv_block(n, nblk - 1, Q_W + KV_W, vprev)
        kprev[n] = k_last
        vprev[n] = v_last
        kl_ref[n] = k_last
        vl_ref[n] = v_last
        gate = proj[n][:, Q_W + 2 * KV_W:]
        h2 = h1_ref[n] + _dot(jnp.concatenate(o_blocks, axis=0) * _silu(gate), wout_ref[...])
        gl = _dot(_rms(h2, vec_ref[V1_PLE:V1_PLE + 1]), wpg_ref[...])
        h2 = h2 + _sigmoid(gl) * pp[n]
        y_ref[n] = _rms(h2, vec_ref[V1_FINAL:V1_FINAL + 1])

    order = [(n, j) for n in streams for j in range(nblk)]
    o_blocks = {n: [] for n in streams}
    pending = None
    for unit in order + [None]:
        scores = scores_of(*unit) if unit is not None else None
        if pending is not None:
            (n, j), sc = pending
            o_blocks[n].append(_attn_values(sc, kv_block(n, j - 1, Q_W + KV_W, vprev),
                                            kv_block(n, j, Q_W + KV_W, vprev), sink_ref, lo))
            if j == nblk - 1:
                finish(n, o_blocks[n])
        pending = (unit, scores)


def _prompt_l1(sinks_p, h1, p_all, vec1, w_in_b, w_out_b, wpg1, wpp1):
    nb, seq, _ = h1.shape
    t = PROMPT_TILE
    return pl.pallas_call(
        _prompt_l1_kernel,
        grid=(seq // t,),
        in_specs=[
            pl.BlockSpec(memory_space=pltpu.SMEM),
            pl.BlockSpec((nb, t, D_MODEL), lambda s: (0, s, 0)),
            pl.BlockSpec((None, nb, t, D_PLE), lambda s: (1, 0, s, 0)),
            _const_spec(vec1.shape), _const_spec(w_in_b.shape), _const_spec(w_out_b.shape),
            _const_spec(wpg1.shape), _const_spec(wpp1.shape),
        ],
        out_specs=[
            pl.BlockSpec((nb, t, D_MODEL), lambda s: (0, s, 0)),
            pl.BlockSpec((nb, BLOCK, KV_W), lambda s: (0, 0, 0)),
            pl.BlockSpec((nb, BLOCK, KV_W), lambda s: (0, 0, 0)),
        ],
        out_shape=[
            jax.ShapeDtypeStruct((nb, seq, D_MODEL), F32),
            jax.ShapeDtypeStruct((nb, BLOCK, KV_W), F32),
            jax.ShapeDtypeStruct((nb, BLOCK, KV_W), F32),
        ],
        scratch_shapes=[
            pltpu.VMEM((nb, BLOCK, KV_W), F32), pltpu.VMEM((nb, BLOCK, KV_W), F32),
            pltpu.VMEM((2, 2 * GROUP, BLOCK, 4 * BLOCK), F32),
        ] + [pltpu.VMEM(w.shape, w.dtype) for w in (w_in_b, w_out_b, wpg1, wpp1)],
        compiler_params=pltpu.CompilerParams(
            dimension_semantics=("arbitrary",), vmem_limit_bytes=VMEM_LIMIT),
        name="prompt_l1",
    )(sinks_p, h1, p_all, vec1, w_in_b, w_out_b, wpg1, wpp1)


def _sample_l0_kernel(x_ref, p_ref, h0_ref, cst_ref, vec_ref, vec1_ref, win_ref, wg_ref, wout_ref, wpg_ref, wpp_ref,
                      winb_ref, h1_ref, hl_ref, cnew_ref, q_ref, k_ref, v_ref, gate_ref):
    nseq = h0_ref.shape[0]
    steps = x_ref.shape[0] // nseq
    x = x_ref[...]
    xg = _dot(_rms(x, vec_ref[V0_NORM:V0_NORM + 1]), win_ref[...])
    xb = xg[:, :D_RNN]
    gate = xg[:, D_RNN:]
    hist = [cst_ref[k] for k in range(CONV_W - 1)] + [xb[t * nseq:(t + 1) * nseq] for t in range(steps)]
    xc = []
    for t in range(steps):
        acc = vec_ref[V0_CB:V0_CB + 1] + vec_ref[V0_CW:V0_CW + 1] * hist[t]
        for k in range(1, CONV_W):
            acc = acc + vec_ref[V0_CW + k:V0_CW + k + 1] * hist[t + k]
        xc.append(acc)
    for k in range(CONV_W - 1):
        cnew_ref[k] = hist[steps + k]
    xc = jnp.concatenate(xc, axis=0)
    a, b = _lru_coeffs(xc, _lru_gate_logits(xc, wg_ref), vec_ref[V0_BR:V0_BR + 1], vec_ref[V0_BI:V0_BI + 1],
                       vec_ref[V0_LAM:V0_LAM + 1])
    h = h0_ref[...]
    hs = []
    for t in range(steps):
        h = a[t * nseq:(t + 1) * nseq] * h + b[t * nseq:(t + 1) * nseq]
        hs.append(h)
    hl_ref[...] = h
    h1 = x + _dot(jnp.concatenate(hs, axis=0) * _silu(gate), wout_ref[...])
    h1 = _ple(h1, p_ref[...], vec_ref[V0_PLE:V0_PLE + 1], wpg_ref, wpp_ref)
    h1_ref[...] = h1
    proj = _dot(_rms(h1, vec1_ref[V1_NORM:V1_NORM + 1]), winb_ref[...])
    q_ref[...] = proj[:, :Q_W] * QK_SCALE
    k_ref[...] = proj[:, Q_W:Q_W + KV_W]
    v_ref[...] = proj[:, Q_W + KV_W:Q_W + 2 * KV_W]
    gate_ref[...] = proj[:, Q_W + 2 * KV_W:]


def _sample_l0(x_tm, p0_tm, h0, cst_tm, vec0, vec1, w_in_a, wg, w_out_a, wpg0, wpp0, w_in_b):
    rows = x_tm.shape[0]
    nseq = h0.shape[0]
    args = (x_tm, p0_tm, h0, cst_tm, vec0, vec1, w_in_a, wg, w_out_a, wpg0, wpp0, w_in_b)
    out_shapes = [(rows, D_MODEL), (nseq, D_RNN), (CONV_W - 1, nseq, D_RNN),
                  (rows, Q_W), (rows, KV_W), (rows, KV_W), (rows, Q_W)]
    return pl.pallas_call(
        _sample_l0_kernel,
        grid=(1,),
        in_specs=[_const_spec(a.shape) for a in args],
        out_specs=[pl.BlockSpec(s, lambda *_, n=len(s): (0,) * n) for s in out_shapes],
        out_shape=[jax.ShapeDtypeStruct(s, F32) for s in out_shapes],
        compiler_params=pltpu.CompilerParams(dimension_semantics=("arbitrary",), vmem_limit_bytes=VMEM_LIMIT),
        name="sample_l0",
    )(*args)


def _sample_attn_kernel(q_ref, knt_ref, vnt_ref, kc_ref, vc_ref, rowinfo_ref, o_ref, nk_ref, nv_ref):
    nseq = q_ref.shape[0]
    nq = q_ref.shape[1]
    steps = nq // GROUP
    rows = N_KV * nq
    sink = rowinfo_ref[:, 0:1]
    slope = rowinfo_ref[:, 1:2]
    tq = rowinfo_ref[:, 2:3]
    col = lax.broadcasted_iota(jnp.int32, (rows, 2 * WINDOW), 1).astype(F32)
    dist = WINDOW + tq - col
    valid = (dist >= 0.0) & (dist < WINDOW)
    bias = jnp.where(valid, -(slope * dist), NEG_INF)
    lane_grp = lax.broadcasted_iota(jnp.int32, (1, KV_W), 1) // HEAD_DIM
    new_lane = lax.broadcasted_iota(jnp.int32, (1, WINDOW), 1) < steps
    knt_all = knt_ref[...]
    vnt_all = vnt_ref[...]

    for bl in range(nseq):
        q = q_ref[bl]
        q_full = jnp.concatenate([jnp.where(lane_grp == g, q, 0.0) for g in range(N_KV)], axis=0).astype(BF16)
        shift = (LANES - steps * bl) % LANES
        knt = knt_all if shift == 0 else pltpu.roll(knt_all, shift, 1)
        vnt = vnt_all if shift == 0 else pltpu.roll(vnt_all, shift, 1)
        kt_ext = jnp.concatenate([kc_ref[bl], jnp.where(new_lane, knt, 0.0)], axis=1)
        vt_ext = jnp.concatenate([vc_ref[bl], jnp.where(new_lane, vnt, 0.0)], axis=1)
        s = jnp.dot(q_full, kt_ext.astype(BF16), preferred_element_type=F32) + bias
        m = jnp.maximum(jnp.max(s, axis=-1, keepdims=True), sink)
        e = jnp.exp(s - m)
        denom = jnp.sum(e, axis=-1, keepdims=True) + jnp.exp(sink - m)
        o = lax.dot_general(e.astype(BF16), vt_ext.astype(BF16), (((1,), (1,)), ((), ())),
                            preferred_element_type=F32) * (1.0 / denom)
        acc = jnp.where(lane_grp == 0, o[:nq], 0.0)
        for g in range(1, N_KV):
            acc = acc + jnp.where(lane_grp == g, o[g * nq:(g + 1) * nq], 0.0)
        o_ref[bl] = acc
        nk_ref[bl] = pltpu.roll(kt_ext, 2 * WINDOW - steps, 1)[:, :WINDOW]
        nv_ref[bl] = pltpu.roll(vt_ext, 2 * WINDOW - steps, 1)[:, :WINDOW]


def _sample_attn(q16, knt, vnt, kc, vc, rowinfo):
    nseq_total, nq, _ = q16.shape
    sb = SAMPLE_SEQS
    blk3 = lambda i: (i, 0, 0)
    return pl.pallas_call(
        _sample_attn_kernel,
        grid=(nseq_total // sb,),
        in_specs=[
            pl.BlockSpec((sb, nq, KV_W), blk3),
            pl.BlockSpec((None, KV_W, LANES), blk3),
            pl.BlockSpec((None, KV_W, LANES), blk3),
            pl.BlockSpec((sb, KV_W, WINDOW), blk3),
            pl.BlockSpec((sb, KV_W, WINDOW), blk3),
            pl.BlockSpec(rowinfo.shape, lambda i: (0, 0)),
        ],
        out_specs=[
            pl.BlockSpec((sb, nq, KV_W), blk3),
            pl.BlockSpec((sb, KV_W, WINDOW), blk3),
            pl.BlockSpec((sb, KV_W, WINDOW), blk3),
        ],
        out_shape=[
            jax.ShapeDtypeStruct((nseq_total, nq, KV_W), F32),
            jax.ShapeDtypeStruct((nseq_total, KV_W, WINDOW), F32),
            jax.ShapeDtypeStruct((nseq_total, KV_W, WINDOW), F32),
        ],
        compiler_params=pltpu.CompilerParams(dimension_semantics=("arbitrary",), vmem_limit_bytes=VMEM_LIMIT),
        name="sample_attn",
    )(q16, knt, vnt, kc, vc, rowinfo)


def _sample_tail_kernel(o_ref, gate_ref, h1_ref, p_ref, vec_ref, wout_ref, wpg_ref, wpp_ref, y_ref):
    o = jnp.concatenate([o_ref[hh] for hh in range(GROUP)], axis=1)
    h1 = h1_ref[...]
    h2 = h1 + _dot(o * _silu(gate_ref[...]), wout_ref[...])
    h2 = _ple(h2, p_ref[...], vec_ref[V1_PLE:V1_PLE + 1], wpg_ref, wpp_ref)
    y_ref[...] = _rms(h2, vec_ref[V1_FINAL:V1_FINAL + 1])


def _sample_tail(o4, gate, h1, p1_tm, vec1, w_out_b, wpg1, wpp1):
    args = (o4, gate, h1, p1_tm, vec1, w_out_b, wpg1, wpp1)
    return pl.pallas_call(
        _sample_tail_kernel,
        grid=(1,),
        in_specs=[_const_spec(a.shape) for a in args],
        out_specs=pl.BlockSpec(h1.shape, lambda i: (0, 0)),
        out_shape=jax.ShapeDtypeStruct(h1.shape, F32),
        compiler_params=pltpu.CompilerParams(dimension_semantics=("arbitrary",), vmem_limit_bytes=VMEM_LIMIT),
        name="sample_tail",
    )(*args)


def _heads_to_slot_major(w, axis):
    shape = w.shape
    w = w.reshape(shape[:axis] + (N_KV, GROUP, HEAD_DIM) + shape[axis + 1:])
    w = jnp.swapaxes(w, axis, axis + 1)
    return w.reshape(shape)


def _pad_rows(v, rows):
    return jnp.concatenate([v, jnp.zeros((rows - v.shape[0], v.shape[1]), v.dtype)], axis=0)


def kernel(x_prompt, x_sample, p_prompt, p_sample, state_lru_h, state_conv, cache_k, cache_v, norm_g, final_norm_g, ple_norm_g, w_ple_gate, w_ple_proj, w_in_a, conv_w_a, conv_b_a, w_rgate, b_rgate, w_igate, b_igate, lru_lambda, w_out_a, w_in_b, sinks, w_out_b):
    assert norm_g.shape[0] == 2 and w_in_a.shape[0] == 1 and w_in_b.shape[0] == 1
    nseq, steps, _ = x_sample.shape

    vec0 = _pad_rows(jnp.concatenate([norm_g[0:1], ple_norm_g[0:1], conv_b_a, b_rgate, b_igate, lru_lambda,
                                      conv_w_a[0]], axis=0), 2 * SUBLANES)
    vec1 = _pad_rows(jnp.concatenate([norm_g[1:2], ple_norm_g[1:2], final_norm_g[None]], axis=0), SUBLANES)
    w_in_a_b = w_in_a[0].astype(BF16)
    w_out_a_b = w_out_a[0].astype(BF16)
    wg = jnp.concatenate([w_rgate[0], w_igate[0]], axis=-1).astype(BF16)
    wpg = w_ple_gate.astype(BF16)
    wpp = w_ple_proj.astype(BF16)
    wb = w_in_b[0]
    w_in_b_b = jnp.concatenate([_heads_to_slot_major(wb[:, :Q_W], 1), wb[:, Q_W:Q_W + 2 * KV_W],
                                _heads_to_slot_major(wb[:, Q_W + 2 * KV_W:], 1)], axis=1).astype(BF16)
    w_out_b_b = _heads_to_slot_major(w_out_b[0], 0).astype(BF16)
    sinks_p = sinks[0].reshape(N_KV, GROUP).T.reshape(1, N_HEADS)

    h1_p, hl_p, tail_p = _prompt_l0(x_prompt, p_prompt, vec0, w_in_a_b, wg, w_out_a_b, wpg[0], wpp[0])
    y_prompt, kl_p, vl_p = _prompt_l1(sinks_p, h1_p, p_prompt, vec1, w_in_b_b, w_out_b_b, wpg[1], wpp[1])
    nbp = x_prompt.shape[0]
    new_h_p = hl_p[:, SUBLANES - 1][None]
    new_conv_p = tail_p[:, SUBLANES - (CONV_W - 1):][None]
    new_k_p = kl_p.reshape(1, nbp, WINDOW, N_KV, HEAD_DIM)
    new_v_p = vl_p.reshape(1, nbp, WINDOW, N_KV, HEAD_DIM)

    rows = steps * nseq
    x_tm = jnp.swapaxes(x_sample, 0, 1).reshape(rows, D_MODEL)
    p_tm = jnp.swapaxes(p_sample, 1, 2).reshape(2, rows, D_PLE)
    cst_tm = jnp.swapaxes(state_conv[0], 0, 1)
    h1_s, hl_s, cnew_tm, q_s, k_s, v_s, gate_s = _sample_l0(
        x_tm, p_tm[0], state_lru_h[0], cst_tm, vec0, vec1, w_in_a_b, wg, w_out_a_b, wpg[0], wpp[0], w_in_b_b)

    sb = SAMPLE_SEQS
    q16 = q_s.reshape(steps, nseq, GROUP, KV_W).transpose(1, 2, 0, 3).reshape(nseq, GROUP * steps, KV_W)

    def new_rows_t(a):
        a = a.reshape(steps, nseq // sb, sb, KV_W).transpose(1, 3, 2, 0).reshape(nseq // sb, KV_W, sb * steps)
        return jnp.concatenate([a, jnp.zeros((nseq // sb, KV_W, LANES - sb * steps), a.dtype)], axis=-1)

    kc = cache_k[0].transpose(0, 2, 3, 1).reshape(nseq, KV_W, WINDOW)
    vc = cache_v[0].transpose(0, 2, 3, 1).reshape(nseq, KV_W, WINDOW)
    head = jnp.arange(N_KV * GROUP * steps) // steps
    slopes = jnp.exp2(-8.0 * (head + 1).astype(F32) / N_HEADS)
    tq = (jnp.arange(N_KV * GROUP * steps) % steps).astype(F32)
    rowinfo = jnp.stack([sinks[0][head], slopes, tq], axis=1)
    rowinfo = jnp.concatenate([rowinfo, jnp.zeros((rowinfo.shape[0], LANES - 3), F32)], axis=1)
    o16, nk, nv = _sample_attn(q16, new_rows_t(k_s), new_rows_t(v_s), kc, vc, rowinfo)

    o4 = o16.reshape(nseq, GROUP, steps, KV_W).transpose(1, 2, 0, 3).reshape(GROUP, rows, KV_W)
    y_tm = _sample_tail(o4, gate_s, h1_s, p_tm[1], vec1, w_out_b_b, wpg[1], wpp[1])
    y_sample = jnp.swapaxes(y_tm.reshape(steps, nseq, D_MODEL), 0, 1)
    new_conv_s = jnp.swapaxes(cnew_tm, 0, 1)[None]
    new_k_s = nk.reshape(nseq, N_KV, HEAD_DIM, WINDOW).transpose(0, 3, 1, 2)[None]
    new_v_s = nv.reshape(nseq, N_KV, HEAD_DIM, WINDOW).transpose(0, 3, 1, 2)[None]

    return (y_prompt, y_sample, new_h_p, new_conv_p, new_k_p, new_v_p,
            hl_s[None], new_conv_s, new_k_s, new_v_s)
```

```python
import functools

import jax
import jax.numpy as jnp
from jax import lax
from jax.experimental import pallas as pl
from jax.experimental.pallas import tpu as pltpu

D_MODEL = 1024
D_RNN = 1024
D_PLE = 256
N_LRU_BLOCKS = 8
LRU_BLOCK = 128
CONV_W = 4
LRU_C = 8.0
EPS = 1e-6
HEAD_DIM = 64
N_HEADS = 16
N_KV = 4
GROUP = 4
Q_W = 1024
KV_W = 256
WINDOW = 128
BLOCK = 128
NEG_INF = -1e30
QK_SCALE = HEAD_DIM ** -0.5
SQRT_FLOOR = 1e-37

LANES = 128
SUBLANES = 8
VMEM_LIMIT = 56 * 1024 * 1024

PROMPT_TILE = 256
SAMPLE_SEQS = 8

F32 = jnp.float32
BF16 = jnp.bfloat16


def _rms(x, g):
    ms = jnp.mean(x * x, axis=-1, keepdims=True)
    return x * lax.rsqrt(ms + EPS) * g


def _sigmoid(x):
    return 0.5 * jnp.tanh(0.5 * x) + 0.5


def _silu(x):
    return x * _sigmoid(x)


def _dot(a, w):
    return jnp.dot(a.astype(BF16), w, preferred_element_type=F32)


def _softplus(z):
    return jnp.maximum(z, 0.0) + jnp.log1p(jnp.exp(-jnp.abs(z)))


def _lru_gate_logits(xc, wg_ref):
    xcb = xc.astype(BF16)
    zr, zi = [], []
    for n in range(N_LRU_BLOCKS):
        z = jnp.dot(xcb[:, n * LRU_BLOCK:(n + 1) * LRU_BLOCK], wg_ref[n], preferred_element_type=F32)
        zr.append(z[:, :LRU_BLOCK])
        zi.append(z[:, LRU_BLOCK:])
    return jnp.concatenate(zr, axis=1), jnp.concatenate(zi, axis=1)


def _lru_coeffs(xc, logits, b_r, b_i, lam):
    r = _sigmoid(logits[0] + b_r)
    i = _sigmoid(logits[1] + b_i)
    log_a = (-LRU_C * _softplus(-lam)) * r
    a = jnp.exp(log_a)
    y = -jnp.tanh(log_a) * (a * a + 1.0)
    b = (y * lax.rsqrt(jnp.maximum(y, SQRT_FLOOR))) * (i * xc)
    return a, b


def _ple(h, p, g, wpg_ref, wpp_ref):
    gate = _sigmoid(_dot(_rms(h, g), wpg_ref[...]))
    return h + gate * _dot(p, wpp_ref[...])


CHUNK = 4
HDR = SUBLANES


def _local_scans(aq, bq):
    big_a, big_b = [aq[0]], [bq[0]]
    for q in range(1, CHUNK):
        big_a.append(aq[q] * big_a[q - 1])
        big_b.append(aq[q] * big_b[q - 1] + bq[q])
    return big_a, big_b


def _carry_in(i, a_tot, b_tot, h0, levels):
    m = a_tot.shape[0]
    if m == CHUNK:
        h, cin = h0, []
        for r in range(CHUNK):
            cin.append(h)
            h = a_tot[r:r + 1] * h + b_tot[r:r + 1]
        return jnp.concatenate(cin, axis=0), h
    a_buf, b_buf, h_buf = levels[0]
    a_buf[i] = a_tot
    b_buf[i] = b_tot
    mq = m // CHUNK
    aq = [a_buf[i, pl.ds(q, mq, stride=CHUNK), :] for q in range(CHUNK)]
    bq = [b_buf[i, pl.ds(q, mq, stride=CHUNK), :] for q in range(CHUNK)]
    big_a, big_b = _local_scans(aq, bq)
    cin, h_last = _carry_in(i, big_a[-1], big_b[-1], h0, levels[1:])
    h_buf[i, HDR - 1:HDR, :] = h0
    for q in range(CHUNK):
        h_buf[i, pl.ds(HDR + q, mq, stride=CHUNK), :] = big_b[q] + big_a[q] * cin
    return h_buf[i, HDR - 1:HDR - 1 + m, :], h_last


V0_NORM, V0_PLE, V0_CB, V0_BR, V0_BI, V0_LAM, V0_CW = 0, 1, 2, 3, 4, 5, 6
V1_NORM, V1_PLE, V1_FINAL = 0, 1, 2


def _prompt_l0_kernel(x_ref, p_ref, vec_ref, win_ref, wg_ref, wout_ref, wpg_ref, wpp_ref,
                      h1_ref, hl_ref, tail_ref,
                      hcar, xb_s, xc_s, zr_s, zi_s, h_s, l2a, l2b, l2h, l3a, l3b, l3h):
    t_rows = x_ref.shape[1]
    nslab = D_RNN // LANES
    levels = ((l2a, l2b, l2h), (l3a, l3b, l3h))

    @pl.when(pl.program_id(0) == 0)
    def _():
        hcar[...] = jnp.zeros_like(hcar)
        xb_s[:, 0:HDR, :] = jnp.zeros((xb_s.shape[0], HDR, LANES), F32)

    st = [dict() for _ in range(x_ref.shape[0])]

    def in_proj(n):
        st[n]["xg"] = _dot(_rms(x_ref[n], vec_ref[V0_NORM:V0_NORM + 1]), win_ref[...])
        st[n]["pp"] = _dot(p_ref[n], wpp_ref[...])

    def conv_gates(n):
        xg = st[n]["xg"]
        tails = []
        for l in range(nslab):
            i = n * nslab + l
            lanes = slice(l * LANES, (l + 1) * LANES)
            xb = xg[:, lanes]
            xb_s[i, HDR:HDR + t_rows, :] = xb
            acc = vec_ref[V0_CB:V0_CB + 1, lanes] + vec_ref[V0_CW + 3:V0_CW + 4, lanes] * xb
            for k in range(1, CONV_W):
                acc = acc + vec_ref[V0_CW + 3 - k:V0_CW + 4 - k, lanes] * xb_s[i, HDR - k:HDR - k + t_rows, :]
            xc_s[i] = acc
            z = jnp.dot(acc.astype(BF16), wg_ref[l], preferred_element_type=F32)
            zr_s[i] = z[:, :LRU_BLOCK]
            zi_s[i] = z[:, LRU_BLOCK:]
            tail = xb[t_rows - HDR:]
            xb_s[i, 0:HDR, :] = tail
            tails.append(tail)
        tail_ref[n] = jnp.concatenate(tails, axis=1)

    def recurrence(n):
        nchunk = t_rows // CHUNK
        for l in range(nslab):
            i = n * nslab + l
            lanes = slice(l * LANES, (l + 1) * LANES)
            aq, bq = [], []
            for q in range(CHUNK):
                rows = pl.ds(q, nchunk, stride=CHUNK)
                a, b = _lru_coeffs(xc_s[i, rows, :], (zr_s[i, rows, :], zi_s[i, rows, :]),
                                   vec_ref[V0_BR:V0_BR + 1, lanes], vec_ref[V0_BI:V0_BI + 1, lanes],
                                   vec_ref[V0_LAM:V0_LAM + 1, lanes])
                aq.append(a)
                bq.append(b)
            big_a, big_b = _local_scans(aq, bq)
            cin, h_last = _carry_in(i, big_a[-1], big_b[-1], hcar[n, :, lanes], levels)
            for q in range(CHUNK):
                h_s[i, pl.ds(q, nchunk, stride=CHUNK), :] = big_b[q] + big_a[q] * cin
            hcar[n, :, lanes] = h_last
        h = jnp.concatenate([h_s[n * nslab + l] for l in range(nslab)], axis=1)
        hl_ref[n] = h[t_rows - SUBLANES:]
        st[n]["hg"] = h * _silu(st[n]["xg"][:, D_RNN:])

    def out_proj(n):
        st[n]["h1"] = x_ref[n] + _dot(st[n]["hg"], wout_ref[...])

    def ple(n):
        h1 = st[n]["h1"]
        gl = _dot(_rms(h1, vec_ref[V0_PLE:V0_PLE + 1]), wpg_ref[...])
        h1_ref[n] = h1 + _sigmoid(gl) * st[n]["pp"]

    for stage in (in_proj, conv_gates):
        for n in range(len(st)):
            stage(n)
    for n in range(len(st)):
        recurrence(n)
        out_proj(n)
    for n in range(len(st)):
        ple(n)


def _const_spec(shape):
    return pl.BlockSpec(shape, lambda *_: (0,) * len(shape), pipeline_mode=pl.Buffered(1))


def _prompt_l0(x, p_all, vec0, w_in_a, wg, w_out_a, wpg0, wpp0):
    nb, seq, _ = x.shape
    t = PROMPT_TILE
    assert t % CHUNK ** 3 == 0 and t // CHUNK ** 3 == CHUNK, "three strided levels, then a CHUNK-row serial scan"
    nslab = nb * (D_RNN // LANES)
    return pl.pallas_call(
        _prompt_l0_kernel,
        grid=(seq // t,),
        in_specs=[
            pl.BlockSpec((nb, t, D_MODEL), lambda s: (0, s, 0)),
            pl.BlockSpec((None, nb, t, D_PLE), lambda s: (0, 0, s, 0)),
            _const_spec(vec0.shape), _const_spec(w_in_a.shape), _const_spec(wg.shape),
            _const_spec(w_out_a.shape), _const_spec(wpg0.shape), _const_spec(wpp0.shape),
        ],
        out_specs=[
            pl.BlockSpec((nb, t, D_MODEL), lambda s: (0, s, 0)),
            pl.BlockSpec((nb, SUBLANES, D_RNN), lambda s: (0, 0, 0)),
            pl.BlockSpec((nb, SUBLANES, D_RNN), lambda s: (0, 0, 0)),
        ],
        out_shape=[
            jax.ShapeDtypeStruct((nb, seq, D_MODEL), F32),
            jax.ShapeDtypeStruct((nb, SUBLANES, D_RNN), F32),
            jax.ShapeDtypeStruct((nb, SUBLANES, D_RNN), F32),
        ],
        scratch_shapes=[pltpu.VMEM((nb, 1, D_RNN), F32), pltpu.VMEM((nslab, HDR + t, LANES), F32)]
        + [pltpu.VMEM((nslab, t, LANES), F32)] * 4
        + [pltpu.VMEM((nslab, rows, LANES), F32)
           for m in (t // CHUNK, t // CHUNK ** 2) for rows in (m, m, HDR + m)],
        compiler_params=pltpu.CompilerParams(
            dimension_semantics=("arbitrary",), vmem_limit_bytes=VMEM_LIMIT),
        name="prompt_l0",
    )(x, p_all, vec0, w_in_a, wg, w_out_a, wpg0, wpp0)


def _slope(head):
    return 2.0 ** (-8.0 * (head + 1) / N_HEADS)


def _build_prompt_bias(bias_ref):
    shape = (BLOCK, 4 * BLOCK)
    qi = lax.broadcasted_iota(jnp.int32, shape, 0)
    col = lax.broadcasted_iota(jnp.int32, shape, 1)
    si = col & (2 * BLOCK - 1)
    second = col >= 2 * BLOCK
    dist = BLOCK + qi - si
    valid = (dist >= 0) & (dist < WINDOW)
    valid_first = valid & (si >= BLOCK)
    distf = dist.astype(F32)
    for hh in range(GROUP):
        for i in range(N_KV // 2):
            slope = jnp.where(second, _slope(GROUP * (2 * i + 1) + hh), _slope(GROUP * (2 * i) + hh))
            b = -(slope * distf)
            bias_ref[0, hh * 2 + i] = jnp.where(valid_first, b, NEG_INF)
            bias_ref[1, hh * 2 + i] = jnp.where(valid, b, NEG_INF)


def _masked_pair(x2, lo):
    return jnp.concatenate([jnp.where(lo, x2, 0.0), jnp.where(lo, 0.0, x2)], axis=0).astype(BF16)


def _attn_scores(q, k_prev, k_cur, bias_ref, tbl, lo):
    scores = []
    for i in range(N_KV // 2):
        lanes = slice(i * LANES, (i + 1) * LANES)
        k_both = _masked_pair(jnp.concatenate([k_prev[:, lanes], k_cur[:, lanes]], axis=0), lo)
        for hh in range(GROUP):
            q2 = q[:, hh * KV_W + i * LANES:hh * KV_W + (i + 1) * LANES]
            s2 = lax.dot_general(q2, k_both, (((1,), (1,)), ((), ())), preferred_element_type=F32)
            scores.append(s2 + bias_ref[tbl, hh * 2 + i])
    return scores


def _attn_values(scores, v_prev, v_cur, sink_ref, lo):
    cols = [None] * (2 * GROUP)
    for i in range(N_KV // 2):
        lanes = slice(i * LANES, (i + 1) * LANES)
        v_both = _masked_pair(jnp.concatenate([v_prev[:, lanes], v_cur[:, lanes]], axis=0), lo)
        for hh in range(GROUP):
            s2 = scores[i * GROUP + hh]
            es, inv = [], []
            for half in range(2):
                sink = sink_ref[0, hh * N_KV + 2 * i + half]
                sh = s2[:, half * 2 * BLOCK:(half + 1) * 2 * BLOCK]
                m = jnp.maximum(jnp.max(sh, axis=-1, keepdims=True), sink)
                e = jnp.exp(sh - m)
                denom = jnp.sum(e, axis=-1, keepdims=True) + jnp.exp(sink - m)
                es.append(e.astype(BF16))
                inv.append(1.0 / denom)
            o2 = jnp.dot(jnp.concatenate(es, axis=1), v_both, preferred_element_type=F32)
            cols[hh * 2 + i] = o2 * jnp.where(lo, inv[0], inv[1])
    return jnp.concatenate(cols, axis=1)


def _prompt_l1_kernel(sink_ref, h1_ref, p_ref, vec_ref, win_ref, wout_ref, wpg_ref, wpp_ref,
                      y_ref, kl_ref, vl_ref, kprev, vprev, bias_ref):
    nstreams, t_rows = h1_ref.shape[0], h1_ref.shape[1]
    nblk = t_rows // BLOCK
    s_idx = pl.program_id(0)

    @pl.when(s_idx == 0)
    def _():
        _build_prompt_bias(bias_ref)
        kprev[...] = jnp.zeros_like(kprev)
        vprev[...] = jnp.zeros_like(vprev)

    lo = lax.broadcasted_iota(jnp.int32, (1, LANES), 1) < HEAD_DIM
    first_tbl = jnp.where(s_idx == 0, 0, 1)
    streams = range(nstreams)
    proj = [_dot(_rms(h1_ref[n], vec_ref[V1_NORM:V1_NORM + 1]), win_ref[...]) for n in streams]
    pp = [_dot(p_ref[n], wpp_ref[...]) for n in streams]

    def kv_block(n, j, base, carry_ref):
        if j < 0:
            return carry_ref[n]
        return proj[n][j * BLOCK:(j + 1) * BLOCK, base:base + KV_W]

    def scores_of(n, j):
        q = (proj[n][j * BLOCK:(j + 1) * BLOCK, :Q_W] * QK_SCALE).astype(BF16)
        return _attn_scores(q, kv_block(n, j - 1, Q_W, kprev), kv_block(n, j, Q_W, kprev), bias_ref,
                            first_tbl if j == 0 else 1, lo)

    def finish(n, o_blocks):
        k_last = kv_block(n, nblk - 1, Q_W, kprev)
        v_last = kv_block(n, nblk - 1, Q_W + KV_W, vprev)
        kprev[n] = k_last
        vprev[n] = v_last
        kl_ref[n] = k_last
        vl_ref[n] = v_last
        gate = proj[n][:, Q_W + 2 * KV_W:]
        h2 = h1_ref[n] + _dot(jnp.concatenate(o_blocks, axis=0) * _silu(gate), wout_ref[...])
        gl = _dot(_rms(h2, vec_ref[V1_PLE:V1_PLE + 1]), wpg_ref[...])
        h2 = h2 + _sigmoid(gl) * pp[n]
        y_ref[n] = _rms(h2, vec_ref[V1_FINAL:V1_FINAL + 1])

    order = [(n, j) for n in streams for j in range(nblk)]
    o_blocks = {n: [] for n in streams}
    pending = None
    for unit in order + [None]:
        scores = scores_of(*unit) if unit is not None else None
        if pending is not None:
            (n, j), sc = pending
            o_blocks[n].append(_attn_values(sc, kv_block(n, j - 1, Q_W + KV_W, vprev),
                                            kv_block(n, j, Q_W + KV_W, vprev), sink_ref, lo))
            if j == nblk - 1:
                finish(n, o_blocks[n])
        pending = (unit, scores)


def _prompt_l1(sinks_p, h1, p_all, vec1, w_in_b, w_out_b, wpg1, wpp1):
    nb, seq, _ = h1.shape
    t = PROMPT_TILE
    return pl.pallas_call(
        _prompt_l1_kernel,
        grid=(seq // t,),
        in_specs=[
            pl.BlockSpec(memory_space=pltpu.SMEM),
            pl.BlockSpec((nb, t, D_MODEL), lambda s: (0, s, 0)),
            pl.BlockSpec((None, nb, t, D_PLE), lambda s: (1, 0, s, 0)),
            _const_spec(vec1.shape), _const_spec(w_in_b.shape), _const_spec(w_out_b.shape),
            _const_spec(wpg1.shape), _const_spec(wpp1.shape),
        ],
        out_specs=[
            pl.BlockSpec((nb, t, D_MODEL), lambda s: (0, s, 0)),
            pl.BlockSpec((nb, BLOCK, KV_W), lambda s: (0, 0, 0)),
            pl.BlockSpec((nb, BLOCK, KV_W), lambda s: (0, 0, 0)),
        ],
        out_shape=[
            jax.ShapeDtypeStruct((nb, seq, D_MODEL), F32),
            jax.ShapeDtypeStruct((nb, BLOCK, KV_W), F32),
            jax.ShapeDtypeStruct((nb, BLOCK, KV_W), F32),
        ],
        scratch_shapes=[
            pltpu.VMEM((nb, BLOCK, KV_W), F32), pltpu.VMEM((nb, BLOCK, KV_W), F32),
            pltpu.VMEM((2, 2 * GROUP, BLOCK, 4 * BLOCK), F32),
        ],
        compiler_params=pltpu.CompilerParams(
            dimension_semantics=("arbitrary",), vmem_limit_bytes=VMEM_LIMIT),
        name="prompt_l1",
    )(sinks_p, h1, p_all, vec1, w_in_b, w_out_b, wpg1, wpp1)


def _sample_l0_kernel(x_ref, p_ref, h0_ref, cst_ref, vec_ref, vec1_ref, win_ref, wg_ref, wout_ref, wpg_ref, wpp_ref,
                      winb_ref, h1_ref, hl_ref, cnew_ref, q_ref, k_ref, v_ref, gate_ref):
    nseq = h0_ref.shape[0]
    steps = x_ref.shape[0] // nseq
    x = x_ref[...]
    xg = _dot(_rms(x, vec_ref[V0_NORM:V0_NORM + 1]), win_ref[...])
    xb = xg[:, :D_RNN]
    gate = xg[:, D_RNN:]
    hist = [cst_ref[k] for k in range(CONV_W - 1)] + [xb[t * nseq:(t + 1) * nseq] for t in range(steps)]
    xc = []
    for t in range(steps):
        acc = vec_ref[V0_CB:V0_CB + 1] + vec_ref[V0_CW:V0_CW + 1] * hist[t]
        for k in range(1, CONV_W):
            acc = acc + vec_ref[V0_CW + k:V0_CW + k + 1] * hist[t + k]
        xc.append(acc)
    for k in range(CONV_W - 1):
        cnew_ref[k] = hist[steps + k]
    xc = jnp.concatenate(xc, axis=0)
    a, b = _lru_coeffs(xc, _lru_gate_logits(xc, wg_ref), vec_ref[V0_BR:V0_BR + 1], vec_ref[V0_BI:V0_BI + 1],
                       vec_ref[V0_LAM:V0_LAM + 1])
    h = h0_ref[...]
    hs = []
    for t in range(steps):
        h = a[t * nseq:(t + 1) * nseq] * h + b[t * nseq:(t + 1) * nseq]
        hs.append(h)
    hl_ref[...] = h
    h1 = x + _dot(jnp.concatenate(hs, axis=0) * _silu(gate), wout_ref[...])
    h1 = _ple(h1, p_ref[...], vec_ref[V0_PLE:V0_PLE + 1], wpg_ref, wpp_ref)
    h1_ref[...] = h1
    proj = _dot(_rms(h1, vec1_ref[V1_NORM:V1_NORM + 1]), winb_ref[...])
    q_ref[...] = proj[:, :Q_W] * QK_SCALE
    k_ref[...] = proj[:, Q_W:Q_W + KV_W]
    v_ref[...] = proj[:, Q_W + KV_W:Q_W + 2 * KV_W]
    gate_ref[...] = proj[:, Q_W + 2 * KV_W:]


def _sample_l0(x_tm, p0_tm, h0, cst_tm, vec0, vec1, w_in_a, wg, w_out_a, wpg0, wpp0, w_in_b):
    rows = x_tm.shape[0]
    nseq = h0.shape[0]
    args = (x_tm, p0_tm, h0, cst_tm, vec0, vec1, w_in_a, wg, w_out_a, wpg0, wpp0, w_in_b)
    out_shapes = [(rows, D_MODEL), (nseq, D_RNN), (CONV_W - 1, nseq, D_RNN),
                  (rows, Q_W), (rows, KV_W), (rows, KV_W), (rows, Q_W)]
    return pl.pallas_call(
        _sample_l0_kernel,
        grid=(1,),
        in_specs=[_const_spec(a.shape) for a in args],
        out_specs=[pl.BlockSpec(s, lambda *_, n=len(s): (0,) * n) for s in out_shapes],
        out_shape=[jax.ShapeDtypeStruct(s, F32) for s in out_shapes],
        compiler_params=pltpu.CompilerParams(dimension_semantics=("arbitrary",), vmem_limit_bytes=VMEM_LIMIT),
        name="sample_l0",
    )(*args)


def _sample_attn_kernel(q_ref, knt_ref, vnt_ref, kc_ref, vc_ref, rowinfo_ref, o_ref, nk_ref, nv_ref):
    nseq = q_ref.shape[0]
    nq = q_ref.shape[1]
    steps = nq // GROUP
    rows = N_KV * nq
    sink = rowinfo_ref[:, 0:1]
    slope = rowinfo_ref[:, 1:2]
    tq = rowinfo_ref[:, 2:3]
    col = lax.broadcasted_iota(jnp.int32, (rows, 2 * WINDOW), 1).astype(F32)
    dist = WINDOW + tq - col
    valid = (dist >= 0.0) & (dist < WINDOW)
    bias = jnp.where(valid, -(slope * dist), NEG_INF)
    lane_grp = lax.broadcasted_iota(jnp.int32, (1, KV_W), 1) // HEAD_DIM
    new_lane = lax.broadcasted_iota(jnp.int32, (1, WINDOW), 1) < steps
    knt_all = knt_ref[...]
    vnt_all = vnt_ref[...]

    for bl in range(nseq):
        q = q_ref[bl]
        q_full = jnp.concatenate([jnp.where(lane_grp == g, q, 0.0) for g in range(N_KV)], axis=0).astype(BF16)
        shift = (LANES - steps * bl) % LANES
        knt = knt_all if shift == 0 else pltpu.roll(knt_all, shift, 1)
        vnt = vnt_all if shift == 0 else pltpu.roll(vnt_all, shift, 1)
        kt_ext = jnp.concatenate([kc_ref[bl], jnp.where(new_lane, knt, 0.0)], axis=1)
        vt_ext = jnp.concatenate([vc_ref[bl], jnp.where(new_lane, vnt, 0.0)], axis=1)
        s = jnp.dot(q_full, kt_ext.astype(BF16), preferred_element_type=F32) + bias
        m = jnp.maximum(jnp.max(s, axis=-1, keepdims=True), sink)
        e = jnp.exp(s - m)
        denom = jnp.sum(e, axis=-1, keepdims=True) + jnp.exp(sink - m)
        o = lax.dot_general(e.astype(BF16), vt_ext.astype(BF16), (((1,), (1,)), ((), ())),
                            preferred_element_type=F32) * (1.0 / denom)
        acc = jnp.where(lane_grp == 0, o[:nq], 0.0)
        for g in range(1, N_KV):
            acc = acc + jnp.where(lane_grp == g, o[g * nq:(g + 1) * nq], 0.0)
        o_ref[bl] = acc
        nk_ref[bl] = pltpu.roll(kt_ext, 2 * WINDOW - steps, 1)[:, :WINDOW]
        nv_ref[bl] = pltpu.roll(vt_ext, 2 * WINDOW - steps, 1)[:, :WINDOW]


def _sample_attn(q16, knt, vnt, kc, vc, rowinfo):
    nseq_total, nq, _ = q16.shape
    sb = SAMPLE_SEQS
    blk3 = lambda i: (i, 0, 0)
    return pl.pallas_call(
        _sample_attn_kernel,
        grid=(nseq_total // sb,),
        in_specs=[
            pl.BlockSpec((sb, nq, KV_W), blk3),
            pl.BlockSpec((None, KV_W, LANES), blk3),
            pl.BlockSpec((None, KV_W, LANES), blk3),
            pl.BlockSpec((sb, KV_W, WINDOW), blk3),
            pl.BlockSpec((sb, KV_W, WINDOW), blk3),
            pl.BlockSpec(rowinfo.shape, lambda i: (0, 0)),
        ],
        out_specs=[
            pl.BlockSpec((sb, nq, KV_W), blk3),
            pl.BlockSpec((sb, KV_W, WINDOW), blk3),
            pl.BlockSpec((sb, KV_W, WINDOW), blk3),
        ],
        out_shape=[
            jax.ShapeDtypeStruct((nseq_total, nq, KV_W), F32),
            jax.ShapeDtypeStruct((nseq_total, KV_W, WINDOW), F32),
            jax.ShapeDtypeStruct((nseq_total, KV_W, WINDOW), F32),
        ],
        compiler_params=pltpu.CompilerParams(dimension_semantics=("arbitrary",), vmem_limit_bytes=VMEM_LIMIT),
        name="sample_attn",
    )(q16, knt, vnt, kc, vc, rowinfo)


def _sample_tail_kernel(o_ref, gate_ref, h1_ref, p_ref, vec_ref, wout_ref, wpg_ref, wpp_ref, y_ref):
    o = jnp.concatenate([o_ref[hh] for hh in range(GROUP)], axis=1)
    h1 = h1_ref[...]
    h2 = h1 + _dot(o * _silu(gate_ref[...]), wout_ref[...])
    h2 = _ple(h2, p_ref[...], vec_ref[V1_PLE:V1_PLE + 1], wpg_ref, wpp_ref)
    y_ref[...] = _rms(h2, vec_ref[V1_FINAL:V1_FINAL + 1])


def _sample_tail(o4, gate, h1, p1_tm, vec1, w_out_b, wpg1, wpp1):
    args = (o4, gate, h1, p1_tm, vec1, w_out_b, wpg1, wpp1)
    return pl.pallas_call(
        _sample_tail_kernel,
        grid=(1,),
        in_specs=[_const_spec(a.shape) for a in args],
        out_specs=pl.BlockSpec(h1.shape, lambda i: (0, 0)),
        out_shape=jax.ShapeDtypeStruct(h1.shape, F32),
        compiler_params=pltpu.CompilerParams(dimension_semantics=("arbitrary",), vmem_limit_bytes=VMEM_LIMIT),
        name="sample_tail",
    )(*args)


def _heads_to_slot_major(w, axis):
    shape = w.shape
    w = w.reshape(shape[:axis] + (N_KV, GROUP, HEAD_DIM) + shape[axis + 1:])
    w = jnp.swapaxes(w, axis, axis + 1)
    return w.reshape(shape)


def _pad_rows(v, rows):
    return jnp.concatenate([v, jnp.zeros((rows - v.shape[0], v.shape[1]), v.dtype)], axis=0)


def kernel(x_prompt, x_sample, p_prompt, p_sample, state_lru_h, state_conv, cache_k, cache_v, norm_g, final_norm_g, ple_norm_g, w_ple_gate, w_ple_proj, w_in_a, conv_w_a, conv_b_a, w_rgate, b_rgate, w_igate, b_igate, lru_lambda, w_out_a, w_in_b, sinks, w_out_b):
    assert norm_g.shape[0] == 2 and w_in_a.shape[0] == 1 and w_in_b.shape[0] == 1
    nseq, steps, _ = x_sample.shape

    vec0 = _pad_rows(jnp.concatenate([norm_g[0:1], ple_norm_g[0:1], conv_b_a, b_rgate, b_igate, lru_lambda,
                                      conv_w_a[0]], axis=0), 2 * SUBLANES)
    vec1 = _pad_rows(jnp.concatenate([norm_g[1:2], ple_norm_g[1:2], final_norm_g[None]], axis=0), SUBLANES)
    w_in_a_b = w_in_a[0].astype(BF16)
    w_out_a_b = w_out_a[0].astype(BF16)
    wg = jnp.concatenate([w_rgate[0], w_igate[0]], axis=-1).astype(BF16)
    wpg = w_ple_gate.astype(BF16)
    wpp = w_ple_proj.astype(BF16)
    wb = w_in_b[0]
    w_in_b_b = jnp.concatenate([_heads_to_slot_major(wb[:, :Q_W], 1), wb[:, Q_W:Q_W + 2 * KV_W],
                                _heads_to_slot_major(wb[:, Q_W + 2 * KV_W:], 1)], axis=1).astype(BF16)
    w_out_b_b = _heads_to_slot_major(w_out_b[0], 0).astype(BF16)
    sinks_p = sinks[0].reshape(N_KV, GROUP).T.reshape(1, N_HEADS)

    h1_p, hl_p, tail_p = _prompt_l0(x_prompt, p_prompt, vec0, w_in_a_b, wg, w_out_a_b, wpg[0], wpp[0])
    y_prompt, kl_p, vl_p = _prompt_l1(sinks_p, h1_p, p_prompt, vec1, w_in_b_b, w_out_b_b, wpg[1], wpp[1])
    nbp = x_prompt.shape[0]
    new_h_p = hl_p[:, SUBLANES - 1][None]
    new_conv_p = tail_p[:, SUBLANES - (CONV_W - 1):][None]
    new_k_p = kl_p.reshape(1, nbp, WINDOW, N_KV, HEAD_DIM)
    new_v_p = vl_p.reshape(1, nbp, WINDOW, N_KV, HEAD_DIM)

    rows = steps * nseq
    x_tm = jnp.swapaxes(x_sample, 0, 1).reshape(rows, D_MODEL)
    p_tm = jnp.swapaxes(p_sample, 1, 2).reshape(2, rows, D_PLE)
    cst_tm = jnp.swapaxes(state_conv[0], 0, 1)
    h1_s, hl_s, cnew_tm, q_s, k_s, v_s, gate_s = _sample_l0(
        x_tm, p_tm[0], state_lru_h[0], cst_tm, vec0, vec1, w_in_a_b, wg, w_out_a_b, wpg[0], wpp[0], w_in_b_b)

    sb = SAMPLE_SEQS
    q16 = q_s.reshape(steps, nseq, GROUP, KV_W).transpose(1, 2, 0, 3).reshape(nseq, GROUP * steps, KV_W)

    def new_rows_t(a):
        a = a.reshape(steps, nseq // sb, sb, KV_W).transpose(1, 3, 2, 0).reshape(nseq // sb, KV_W, sb * steps)
        return jnp.concatenate([a, jnp.zeros((nseq // sb, KV_W, LANES - sb * steps), a.dtype)], axis=-1)

    kc = cache_k[0].transpose(0, 2, 3, 1).reshape(nseq, KV_W, WINDOW)
    vc = cache_v[0].transpose(0, 2, 3, 1).reshape(nseq, KV_W, WINDOW)
    head = jnp.arange(N_KV * GROUP * steps) // steps
    slopes = jnp.exp2(-8.0 * (head + 1).astype(F32) / N_HEADS)
    tq = (jnp.arange(N_KV * GROUP * steps) % steps).astype(F32)
    rowinfo = jnp.stack([sinks[0][head], slopes, tq], axis=1)
    rowinfo = jnp.concatenate([rowinfo, jnp.zeros((rowinfo.shape[0], LANES - 3), F32)], axis=1)
    o16, nk, nv = _sample_attn(q16, new_rows_t(k_s), new_rows_t(v_s), kc, vc, rowinfo)

    o4 = o16.reshape(nseq, GROUP, steps, KV_W).transpose(1, 2, 0, 3).reshape(GROUP, rows, KV_W)
    y_tm = _sample_tail(o4, gate_s, h1_s, p_tm[1], vec1, w_out_b_b, wpg[1], wpp[1])
    y_sample = jnp.swapaxes(y_tm.reshape(steps, nseq, D_MODEL), 0, 1)
    new_conv_s = jnp.swapaxes(cnew_tm, 0, 1)[None]
    new_k_s = nk.reshape(nseq, N_KV, HEAD_DIM, WINDOW).transpose(0, 3, 1, 2)[None]
    new_v_s = nv.reshape(nseq, N_KV, HEAD_DIM, WINDOW).transpose(0, 3, 1, 2)[None]

    return (y_prompt, y_sample, new_h_p, new_conv_p, new_k_p, new_v_p,
            hl_s[None], new_conv_s, new_k_s, new_v_s)
```

```python
import functools

import jax
import jax.numpy as jnp
from jax import lax
from jax.experimental import pallas as pl
from jax.experimental.pallas import tpu as pltpu

D_MODEL = 1024
D_RNN = 1024
D_PLE = 256
N_LRU_BLOCKS = 8
LRU_BLOCK = 128
CONV_W = 4
LRU_C = 8.0
EPS = 1e-6
HEAD_DIM = 64
N_HEADS = 16
N_KV = 4
GROUP = 4
Q_W = 1024
KV_W = 256
WINDOW = 128
BLOCK = 128
NEG_INF = -1e30
QK_SCALE = HEAD_DIM ** -0.5
SQRT_FLOOR = 1e-37

LANES = 128
SUBLANES = 8
VMEM_LIMIT = 56 * 1024 * 1024

PROMPT_TILE = 256
SAMPLE_SEQS = 8
ATTN_LAG = 4

F32 = jnp.float32
BF16 = jnp.bfloat16


def _rms(x, g):
    ms = jnp.mean(x * x, axis=-1, keepdims=True)
    return x * lax.rsqrt(ms + EPS) * g


def _sigmoid(x):
    return 0.5 * jnp.tanh(0.5 * x) + 0.5


def _silu(x):
    return x * _sigmoid(x)


def _dot(a, w):
    return jnp.dot(a.astype(BF16), w, preferred_element_type=F32)


def _softplus(z):
    return jnp.maximum(z, 0.0) + jnp.log1p(jnp.exp(-jnp.abs(z)))


def _lru_gate_logits(xc, wg_ref):
    xcb = xc.astype(BF16)
    zr, zi = [], []
    for n in range(N_LRU_BLOCKS):
        z = jnp.dot(xcb[:, n * LRU_BLOCK:(n + 1) * LRU_BLOCK], wg_ref[n], preferred_element_type=F32)
        zr.append(z[:, :LRU_BLOCK])
        zi.append(z[:, LRU_BLOCK:])
    return jnp.concatenate(zr, axis=1), jnp.concatenate(zi, axis=1)


def _lru_coeffs(xc, logits, b_r, b_i, lam):
    r = _sigmoid(logits[0] + b_r)
    i = _sigmoid(logits[1] + b_i)
    log_a = (-LRU_C * _softplus(-lam)) * r
    a = jnp.exp(log_a)
    y = -jnp.tanh(log_a) * (a * a + 1.0)
    b = (y * lax.rsqrt(jnp.maximum(y, SQRT_FLOOR))) * (i * xc)
    return a, b


def _ple(h, p, g, wpg_ref, wpp_ref):
    gate = _sigmoid(_dot(_rms(h, g), wpg_ref[...]))
    return h + gate * _dot(p, wpp_ref[...])


CHUNK = 4
HDR = SUBLANES


def _local_scans(aq, bq):
    big_a, big_b = [aq[0]], [bq[0]]
    for q in range(1, CHUNK):
        big_a.append(aq[q] * big_a[q - 1])
        big_b.append(aq[q] * big_b[q - 1] + bq[q])
    return big_a, big_b


def _carry_in(i, a_tot, b_tot, h0, levels):
    m = a_tot.shape[0]
    if m == CHUNK:
        h, cin = h0, []
        for r in range(CHUNK):
            cin.append(h)
            h = a_tot[r:r + 1] * h + b_tot[r:r + 1]
        return jnp.concatenate(cin, axis=0), h
    a_buf, b_buf, h_buf = levels[0]
    a_buf[i] = a_tot
    b_buf[i] = b_tot
    mq = m // CHUNK
    aq = [a_buf[i, pl.ds(q, mq, stride=CHUNK), :] for q in range(CHUNK)]
    bq = [b_buf[i, pl.ds(q, mq, stride=CHUNK), :] for q in range(CHUNK)]
    big_a, big_b = _local_scans(aq, bq)
    cin, h_last = _carry_in(i, big_a[-1], big_b[-1], h0, levels[1:])
    h_buf[i, HDR - 1:HDR, :] = h0
    for q in range(CHUNK):
        h_buf[i, pl.ds(HDR + q, mq, stride=CHUNK), :] = big_b[q] + big_a[q] * cin
    return h_buf[i, HDR - 1:HDR - 1 + m, :], h_last


V0_NORM, V0_PLE, V0_CB, V0_BR, V0_BI, V0_LAM, V0_CW = 0, 1, 2, 3, 4, 5, 6
V1_NORM, V1_PLE, V1_FINAL = 0, 1, 2


def _prompt_l0_kernel(x_ref, p_ref, vec_ref, win_ref, wg_ref, wout_ref, wpg_ref, wpp_ref,
                      h1_ref, hl_ref, tail_ref,
                      hcar, xb_s, xc_s, zr_s, zi_s, h_s, l2a, l2b, l2h, l3a, l3b, l3h):
    t_rows = x_ref.shape[1]
    nslab = D_RNN // LANES
    levels = ((l2a, l2b, l2h), (l3a, l3b, l3h))

    @pl.when(pl.program_id(0) == 0)
    def _():
        hcar[...] = jnp.zeros_like(hcar)
        xb_s[:, 0:HDR, :] = jnp.zeros((xb_s.shape[0], HDR, LANES), F32)

    st = [dict() for _ in range(x_ref.shape[0])]

    def in_proj(n):
        st[n]["xg"] = _dot(_rms(x_ref[n], vec_ref[V0_NORM:V0_NORM + 1]), win_ref[...])
        st[n]["pp"] = _dot(p_ref[n], wpp_ref[...])

    def conv_gates(n):
        xg = st[n]["xg"]
        tails = []
        for l in range(nslab):
            i = n * nslab + l
            lanes = slice(l * LANES, (l + 1) * LANES)
            xb = xg[:, lanes]
            xb_s[i, HDR:HDR + t_rows, :] = xb
            acc = vec_ref[V0_CB:V0_CB + 1, lanes] + vec_ref[V0_CW + 3:V0_CW + 4, lanes] * xb
            for k in range(1, CONV_W):
                acc = acc + vec_ref[V0_CW + 3 - k:V0_CW + 4 - k, lanes] * xb_s[i, HDR - k:HDR - k + t_rows, :]
            xc_s[i] = acc
            z = jnp.dot(acc.astype(BF16), wg_ref[l], preferred_element_type=F32)
            zr_s[i] = z[:, :LRU_BLOCK]
            zi_s[i] = z[:, LRU_BLOCK:]
            tail = xb[t_rows - HDR:]
            xb_s[i, 0:HDR, :] = tail
            tails.append(tail)
        tail_ref[n] = jnp.concatenate(tails, axis=1)

    def recurrence(n):
        nchunk = t_rows // CHUNK
        for l in range(nslab):
            i = n * nslab + l
            lanes = slice(l * LANES, (l + 1) * LANES)
            aq, bq = [], []
            for q in range(CHUNK):
                rows = pl.ds(q, nchunk, stride=CHUNK)
                a, b = _lru_coeffs(xc_s[i, rows, :], (zr_s[i, rows, :], zi_s[i, rows, :]),
                                   vec_ref[V0_BR:V0_BR + 1, lanes], vec_ref[V0_BI:V0_BI + 1, lanes],
                                   vec_ref[V0_LAM:V0_LAM + 1, lanes])
                aq.append(a)
                bq.append(b)
            big_a, big_b = _local_scans(aq, bq)
            cin, h_last = _carry_in(i, big_a[-1], big_b[-1], hcar[n, :, lanes], levels)
            for q in range(CHUNK):
                h_s[i, pl.ds(q, nchunk, stride=CHUNK), :] = big_b[q] + big_a[q] * cin
            hcar[n, :, lanes] = h_last
        h = jnp.concatenate([h_s[n * nslab + l] for l in range(nslab)], axis=1)
        hl_ref[n] = h[t_rows - SUBLANES:]
        st[n]["hg"] = h * _silu(st[n]["xg"][:, D_RNN:])

    def out_proj(n):
        st[n]["h1"] = x_ref[n] + _dot(st[n]["hg"], wout_ref[...])

    def ple(n):
        h1 = st[n]["h1"]
        gl = _dot(_rms(h1, vec_ref[V0_PLE:V0_PLE + 1]), wpg_ref[...])
        h1_ref[n] = h1 + _sigmoid(gl) * st[n]["pp"]

    for stage in (in_proj, conv_gates):
        for n in range(len(st)):
            stage(n)
    for n in range(len(st)):
        recurrence(n)
        out_proj(n)
    for n in range(len(st)):
        ple(n)


def _const_spec(shape):
    return pl.BlockSpec(shape, lambda *_: (0,) * len(shape), pipeline_mode=pl.Buffered(1))


def _prompt_l0(x, p_all, vec0, w_in_a, wg, w_out_a, wpg0, wpp0):
    nb, seq, _ = x.shape
    t = PROMPT_TILE
    assert t % CHUNK ** 3 == 0 and t // CHUNK ** 3 == CHUNK, "three strided levels, then a CHUNK-row serial scan"
    nslab = nb * (D_RNN // LANES)
    return pl.pallas_call(
        _prompt_l0_kernel,
        grid=(seq // t,),
        in_specs=[
            pl.BlockSpec((nb, t, D_MODEL), lambda s: (0, s, 0)),
            pl.BlockSpec((None, nb, t, D_PLE), lambda s: (0, 0, s, 0)),
            _const_spec(vec0.shape), _const_spec(w_in_a.shape), _const_spec(wg.shape),
            _const_spec(w_out_a.shape), _const_spec(wpg0.shape), _const_spec(wpp0.shape),
        ],
        out_specs=[
            pl.BlockSpec((nb, t, D_MODEL), lambda s: (0, s, 0)),
            pl.BlockSpec((nb, SUBLANES, D_RNN), lambda s: (0, 0, 0)),
            pl.BlockSpec((nb, SUBLANES, D_RNN), lambda s: (0, 0, 0)),
        ],
        out_shape=[
            jax.ShapeDtypeStruct((nb, seq, D_MODEL), F32),
            jax.ShapeDtypeStruct((nb, SUBLANES, D_RNN), F32),
            jax.ShapeDtypeStruct((nb, SUBLANES, D_RNN), F32),
        ],
        scratch_shapes=[pltpu.VMEM((nb, 1, D_RNN), F32), pltpu.VMEM((nslab, HDR + t, LANES), F32)]
        + [pltpu.VMEM((nslab, t, LANES), F32)] * 4
        + [pltpu.VMEM((nslab, rows, LANES), F32)
           for m in (t // CHUNK, t // CHUNK ** 2) for rows in (m, m, HDR + m)],
        compiler_params=pltpu.CompilerParams(
            dimension_semantics=("arbitrary",), vmem_limit_bytes=VMEM_LIMIT),
        name="prompt_l0",
    )(x, p_all, vec0, w_in_a, wg, w_out_a, wpg0, wpp0)


def _slope(head):
    return 2.0 ** (-8.0 * (head + 1) / N_HEADS)


def _build_prompt_bias(bias_ref):
    shape = (BLOCK, 4 * BLOCK)
    qi = lax.broadcasted_iota(jnp.int32, shape, 0)
    col = lax.broadcasted_iota(jnp.int32, shape, 1)
    si = col & (2 * BLOCK - 1)
    second = col >= 2 * BLOCK
    dist = BLOCK + qi - si
    valid = (dist >= 0) & (dist < WINDOW)
    valid_first = valid & (si >= BLOCK)
    distf = dist.astype(F32)
    for hh in range(GROUP):
        for i in range(N_KV // 2):
            slope = jnp.where(second, _slope(GROUP * (2 * i + 1) + hh), _slope(GROUP * (2 * i) + hh))
            b = -(slope * distf)
            bias_ref[0, hh * 2 + i] = jnp.where(valid_first, b, NEG_INF)
            bias_ref[1, hh * 2 + i] = jnp.where(valid, b, NEG_INF)


def _masked_pair(x2, lo):
    return jnp.concatenate([jnp.where(lo, x2, 0.0), jnp.where(lo, 0.0, x2)], axis=0).astype(BF16)


def _pair_scores(q2, k_both, bias):
    return lax.dot_general(q2, k_both, (((1,), (1,)), ((), ())), preferred_element_type=F32) + bias


def _pair_values(s2, v_both, sinks, lo):
    es, inv = [], []
    for half in range(2):
        sh = s2[:, half * 2 * BLOCK:(half + 1) * 2 * BLOCK]
        m = jnp.maximum(jnp.max(sh, axis=-1, keepdims=True), sinks[half])
        e = jnp.exp(sh - m)
        denom = jnp.sum(e, axis=-1, keepdims=True) + jnp.exp(sinks[half] - m)
        es.append(e.astype(BF16))
        inv.append(1.0 / denom)
    o2 = jnp.dot(jnp.concatenate(es, axis=1), v_both, preferred_element_type=F32)
    return o2 * jnp.where(lo, inv[0], inv[1])


def _prompt_l1_kernel(sink_ref, h1_ref, p_ref, vec_ref, win_ref, wout_ref, wpg_ref, wpp_ref,
                      y_ref, kl_ref, vl_ref, kprev, vprev, bias_ref):
    nstreams, t_rows = h1_ref.shape[0], h1_ref.shape[1]
    nblk = t_rows // BLOCK
    s_idx = pl.program_id(0)

    @pl.when(s_idx == 0)
    def _():
        _build_prompt_bias(bias_ref)
        kprev[...] = jnp.zeros_like(kprev)
        vprev[...] = jnp.zeros_like(vprev)

    lo = lax.broadcasted_iota(jnp.int32, (1, LANES), 1) < HEAD_DIM
    first_tbl = jnp.where(s_idx == 0, 0, 1)
    streams = range(nstreams)
    proj = [_dot(_rms(h1_ref[n], vec_ref[V1_NORM:V1_NORM + 1]), win_ref[...]) for n in streams]
    pp = [_dot(p_ref[n], wpp_ref[...]) for n in streams]

    def kv_block(n, j, base, carry_ref):
        if j < 0:
            return carry_ref[n]
        return proj[n][j * BLOCK:(j + 1) * BLOCK, base:base + KV_W]

    cache = {}

    def masked_kv(n, j, i, base, carry_ref):
        key = (n, j, i, base)
        if key not in cache:
            lanes = slice(i * LANES, (i + 1) * LANES)
            prev, cur = kv_block(n, j - 1, base, carry_ref), kv_block(n, j, base, carry_ref)
            cache[key] = _masked_pair(jnp.concatenate([prev[:, lanes], cur[:, lanes]], axis=0), lo)
        return cache[key]

    def scores_of(n, j, i, hh):
        cols = slice(hh * KV_W + i * LANES, hh * KV_W + (i + 1) * LANES)
        q2 = (proj[n][j * BLOCK:(j + 1) * BLOCK, cols] * QK_SCALE).astype(BF16)
        return _pair_scores(q2, masked_kv(n, j, i, Q_W, kprev), bias_ref[first_tbl if j == 0 else 1, hh * 2 + i])

    def values_of(n, j, i, hh, s2):
        sinks = [sink_ref[0, hh * N_KV + 2 * i + half] for half in range(2)]
        return _pair_values(s2, masked_kv(n, j, i, Q_W + KV_W, vprev), sinks, lo)

    def finish(n, o_blocks):
        k_last = kv_block(n, nblk - 1, Q_W, kprev)
        v_last = kv_block(n, nblk - 1, Q_W + KV_W, vprev)
        kprev[n] = k_last
        vprev[n] = v_last
        kl_ref[n] = k_last
        vl_ref[n] = v_last
        gate = proj[n][:, Q_W + 2 * KV_W:]
        h2 = h1_ref[n] + _dot(jnp.concatenate(o_blocks, axis=0) * _silu(gate), wout_ref[...])
        gl = _dot(_rms(h2, vec_ref[V1_PLE:V1_PLE + 1]), wpg_ref[...])
        h2 = h2 + _sigmoid(gl) * pp[n]
        y_ref[n] = _rms(h2, vec_ref[V1_FINAL:V1_FINAL + 1])

    units = [(n, j, i, hh) for n in streams for j in range(nblk) for i in range(N_KV // 2) for hh in range(GROUP)]
    per_block = (N_KV // 2) * GROUP
    scores, cols = {}, {}
    o_blocks = {n: [] for n in streams}
    for t in range(len(units) + ATTN_LAG):
        if t < len(units):
            scores[t] = scores_of(*units[t])
        if t >= ATTN_LAG:
            n, j, i, hh = units[t - ATTN_LAG]
            cols[hh * 2 + i] = values_of(n, j, i, hh, scores.pop(t - ATTN_LAG))
            if len(cols) == per_block:
                o_blocks[n].append(jnp.concatenate([cols[c] for c in range(per_block)], axis=1))
                cols = {}
                if j == nblk - 1:
                    finish(n, o_blocks[n])


def _prompt_l1(sinks_p, h1, p_all, vec1, w_in_b, w_out_b, wpg1, wpp1):
    nb, seq, _ = h1.shape
    t = PROMPT_TILE
    return pl.pallas_call(
        _prompt_l1_kernel,
        grid=(seq // t,),
        in_specs=[
            pl.BlockSpec(memory_space=pltpu.SMEM),
            pl.BlockSpec((nb, t, D_MODEL), lambda s: (0, s, 0)),
            pl.BlockSpec((None, nb, t, D_PLE), lambda s: (1, 0, s, 0)),
            _const_spec(vec1.shape), _const_spec(w_in_b.shape), _const_spec(w_out_b.shape),
            _const_spec(wpg1.shape), _const_spec(wpp1.shape),
        ],
        out_specs=[
            pl.BlockSpec((nb, t, D_MODEL), lambda s: (0, s, 0)),
            pl.BlockSpec((nb, BLOCK, KV_W), lambda s: (0, 0, 0)),
            pl.BlockSpec((nb, BLOCK, KV_W), lambda s: (0, 0, 0)),
        ],
        out_shape=[
            jax.ShapeDtypeStruct((nb, seq, D_MODEL), F32),
            jax.ShapeDtypeStruct((nb, BLOCK, KV_W), F32),
            jax.ShapeDtypeStruct((nb, BLOCK, KV_W), F32),
        ],
        scratch_shapes=[
            pltpu.VMEM((nb, BLOCK, KV_W), F32), pltpu.VMEM((nb, BLOCK, KV_W), F32),
            pltpu.VMEM((2, 2 * GROUP, BLOCK, 4 * BLOCK), F32),
        ],
        compiler_params=pltpu.CompilerParams(
            dimension_semantics=("arbitrary",), vmem_limit_bytes=VMEM_LIMIT),
        name="prompt_l1",
    )(sinks_p, h1, p_all, vec1, w_in_b, w_out_b, wpg1, wpp1)


def _sample_l0_kernel(x_ref, p_ref, h0_ref, cst_ref, vec_ref, vec1_ref, win_ref, wg_ref, wout_ref, wpg_ref, wpp_ref,
                      winb_ref, h1_ref, hl_ref, cnew_ref, q_ref, k_ref, v_ref, gate_ref):
    nseq = h0_ref.shape[0]
    steps = x_ref.shape[0] // nseq
    x = x_ref[...]
    xg = _dot(_rms(x, vec_ref[V0_NORM:V0_NORM + 1]), win_ref[...])
    xb = xg[:, :D_RNN]
    gate = xg[:, D_RNN:]
    hist = [cst_ref[k] for k in range(CONV_W - 1)] + [xb[t * nseq:(t + 1) * nseq] for t in range(steps)]
    xc = []
    for t in range(steps):
        acc = vec_ref[V0_CB:V0_CB + 1] + vec_ref[V0_CW:V0_CW + 1] * hist[t]
        for k in range(1, CONV_W):
            acc = acc + vec_ref[V0_CW + k:V0_CW + k + 1] * hist[t + k]
        xc.append(acc)
    for k in range(CONV_W - 1):
        cnew_ref[k] = hist[steps + k]
    xc = jnp.concatenate(xc, axis=0)
    a, b = _lru_coeffs(xc, _lru_gate_logits(xc, wg_ref), vec_ref[V0_BR:V0_BR + 1], vec_ref[V0_BI:V0_BI + 1],
                       vec_ref[V0_LAM:V0_LAM + 1])
    h = h0_ref[...]
    hs = []
    for t in range(steps):
        h = a[t * nseq:(t + 1) * nseq] * h + b[t * nseq:(t + 1) * nseq]
        hs.append(h)
    hl_ref[...] = h
    h1 = x + _dot(jnp.concatenate(hs, axis=0) * _silu(gate), wout_ref[...])
    h1 = _ple(h1, p_ref[...], vec_ref[V0_PLE:V0_PLE + 1], wpg_ref, wpp_ref)
    h1_ref[...] = h1
    proj = _dot(_rms(h1, vec1_ref[V1_NORM:V1_NORM + 1]), winb_ref[...])
    q_ref[...] = proj[:, :Q_W] * QK_SCALE
    k_ref[...] = proj[:, Q_W:Q_W + KV_W]
    v_ref[...] = proj[:, Q_W + KV_W:Q_W + 2 * KV_W]
    gate_ref[...] = proj[:, Q_W + 2 * KV_W:]


def _sample_l0(x_tm, p0_tm, h0, cst_tm, vec0, vec1, w_in_a, wg, w_out_a, wpg0, wpp0, w_in_b):
    rows = x_tm.shape[0]
    nseq = h0.shape[0]
    args = (x_tm, p0_tm, h0, cst_tm, vec0, vec1, w_in_a, wg, w_out_a, wpg0, wpp0, w_in_b)
    out_shapes = [(rows, D_MODEL), (nseq, D_RNN), (CONV_W - 1, nseq, D_RNN),
                  (rows, Q_W), (rows, KV_W), (rows, KV_W), (rows, Q_W)]
    return pl.pallas_call(
        _sample_l0_kernel,
        grid=(1,),
        in_specs=[_const_spec(a.shape) for a in args],
        out_specs=[pl.BlockSpec(s, lambda *_, n=len(s): (0,) * n) for s in out_shapes],
        out_shape=[jax.ShapeDtypeStruct(s, F32) for s in out_shapes],
        compiler_params=pltpu.CompilerParams(dimension_semantics=("arbitrary",), vmem_limit_bytes=VMEM_LIMIT),
        name="sample_l0",
    )(*args)


def _sample_attn_kernel(q_ref, knt_ref, vnt_ref, kc_ref, vc_ref, rowinfo_ref, o_ref, nk_ref, nv_ref):
    nseq = q_ref.shape[0]
    nq = q_ref.shape[1]
    steps = nq // GROUP
    rows = N_KV * nq
    sink = rowinfo_ref[:, 0:1]
    slope = rowinfo_ref[:, 1:2]
    tq = rowinfo_ref[:, 2:3]
    col = lax.broadcasted_iota(jnp.int32, (rows, 2 * WINDOW), 1).astype(F32)
    dist = WINDOW + tq - col
    valid = (dist >= 0.0) & (dist < WINDOW)
    bias = jnp.where(valid, -(slope * dist), NEG_INF)
    lane_grp = lax.broadcasted_iota(jnp.int32, (1, KV_W), 1) // HEAD_DIM
    new_lane = lax.broadcasted_iota(jnp.int32, (1, WINDOW), 1) < steps
    knt_all = knt_ref[...]
    vnt_all = vnt_ref[...]

    for bl in range(nseq):
        q = q_ref[bl]
        q_full = jnp.concatenate([jnp.where(lane_grp == g, q, 0.0) for g in range(N_KV)], axis=0).astype(BF16)
        shift = (LANES - steps * bl) % LANES
        knt = knt_all if shift == 0 else pltpu.roll(knt_all, shift, 1)
        vnt = vnt_all if shift == 0 else pltpu.roll(vnt_all, shift, 1)
        kt_ext = jnp.concatenate([kc_ref[bl], jnp.where(new_lane, knt, 0.0)], axis=1)
        vt_ext = jnp.concatenate([vc_ref[bl], jnp.where(new_lane, vnt, 0.0)], axis=1)
        s = jnp.dot(q_full, kt_ext.astype(BF16), preferred_element_type=F32) + bias
        m = jnp.maximum(jnp.max(s, axis=-1, keepdims=True), sink)
        e = jnp.exp(s - m)
        denom = jnp.sum(e, axis=-1, keepdims=True) + jnp.exp(sink - m)
        o = lax.dot_general(e.astype(BF16), vt_ext.astype(BF16), (((1,), (1,)), ((), ())),
                            preferred_element_type=F32) * (1.0 / denom)
        acc = jnp.where(lane_grp == 0, o[:nq], 0.0)
        for g in range(1, N_KV):
            acc = acc + jnp.where(lane_grp == g, o[g * nq:(g + 1) * nq], 0.0)
        o_ref[bl] = acc
        nk_ref[bl] = pltpu.roll(kt_ext, 2 * WINDOW - steps, 1)[:, :WINDOW]
        nv_ref[bl] = pltpu.roll(vt_ext, 2 * WINDOW - steps, 1)[:, :WINDOW]


def _sample_attn(q16, knt, vnt, kc, vc, rowinfo):
    nseq_total, nq, _ = q16.shape
    sb = SAMPLE_SEQS
    blk3 = lambda i: (i, 0, 0)
    return pl.pallas_call(
        _sample_attn_kernel,
        grid=(nseq_total // sb,),
        in_specs=[
            pl.BlockSpec((sb, nq, KV_W), blk3),
            pl.BlockSpec((None, KV_W, LANES), blk3),
            pl.BlockSpec((None, KV_W, LANES), blk3),
            pl.BlockSpec((sb, KV_W, WINDOW), blk3),
            pl.BlockSpec((sb, KV_W, WINDOW), blk3),
            pl.BlockSpec(rowinfo.shape, lambda i: (0, 0)),
        ],
        out_specs=[
            pl.BlockSpec((sb, nq, KV_W), blk3),
            pl.BlockSpec((sb, KV_W, WINDOW), blk3),
            pl.BlockSpec((sb, KV_W, WINDOW), blk3),
        ],
        out_shape=[
            jax.ShapeDtypeStruct((nseq_total, nq, KV_W), F32),
            jax.ShapeDtypeStruct((nseq_total, KV_W, WINDOW), F32),
            jax.ShapeDtypeStruct((nseq_total, KV_W, WINDOW), F32),
        ],
        compiler_params=pltpu.CompilerParams(dimension_semantics=("arbitrary",), vmem_limit_bytes=VMEM_LIMIT),
        name="sample_attn",
    )(q16, knt, vnt, kc, vc, rowinfo)


def _sample_tail_kernel(o_ref, gate_ref, h1_ref, p_ref, vec_ref, wout_ref, wpg_ref, wpp_ref, y_ref):
    o = jnp.concatenate([o_ref[hh] for hh in range(GROUP)], axis=1)
    h1 = h1_ref[...]
    h2 = h1 + _dot(o * _silu(gate_ref[...]), wout_ref[...])
    h2 = _ple(h2, p_ref[...], vec_ref[V1_PLE:V1_PLE + 1], wpg_ref, wpp_ref)
    y_ref[...] = _rms(h2, vec_ref[V1_FINAL:V1_FINAL + 1])


def _sample_tail(o4, gate, h1, p1_tm, vec1, w_out_b, wpg1, wpp1):
    args = (o4, gate, h1, p1_tm, vec1, w_out_b, wpg1, wpp1)
    return pl.pallas_call(
        _sample_tail_kernel,
        grid=(1,),
        in_specs=[_const_spec(a.shape) for a in args],
        out_specs=pl.BlockSpec(h1.shape, lambda i: (0, 0)),
        out_shape=jax.ShapeDtypeStruct(h1.shape, F32),
        compiler_params=pltpu.CompilerParams(dimension_semantics=("arbitrary",), vmem_limit_bytes=VMEM_LIMIT),
        name="sample_tail",
    )(*args)


def _heads_to_slot_major(w, axis):
    shape = w.shape
    w = w.reshape(shape[:axis] + (N_KV, GROUP, HEAD_DIM) + shape[axis + 1:])
    w = jnp.swapaxes(w, axis, axis + 1)
    return w.reshape(shape)


def _pad_rows(v, rows):
    return jnp.concatenate([v, jnp.zeros((rows - v.shape[0], v.shape[1]), v.dtype)], axis=0)


def kernel(x_prompt, x_sample, p_prompt, p_sample, state_lru_h, state_conv, cache_k, cache_v, norm_g, final_norm_g, ple_norm_g, w_ple_gate, w_ple_proj, w_in_a, conv_w_a, conv_b_a, w_rgate, b_rgate, w_igate, b_igate, lru_lambda, w_out_a, w_in_b, sinks, w_out_b):
    assert norm_g.shape[0] == 2 and w_in_a.shape[0] == 1 and w_in_b.shape[0] == 1
    nseq, steps, _ = x_sample.shape

    vec0 = _pad_rows(jnp.concatenate([norm_g[0:1], ple_norm_g[0:1], conv_b_a, b_rgate, b_igate, lru_lambda,
                                      conv_w_a[0]], axis=0), 2 * SUBLANES)
    vec1 = _pad_rows(jnp.concatenate([norm_g[1:2], ple_norm_g[1:2], final_norm_g[None]], axis=0), SUBLANES)
    w_in_a_b = w_in_a[0].astype(BF16)
    w_out_a_b = w_out_a[0].astype(BF16)
    wg = jnp.concatenate([w_rgate[0], w_igate[0]], axis=-1).astype(BF16)
    wpg = w_ple_gate.astype(BF16)
    wpp = w_ple_proj.astype(BF16)
    wb = w_in_b[0]
    w_in_b_b = jnp.concatenate([_heads_to_slot_major(wb[:, :Q_W], 1), wb[:, Q_W:Q_W + 2 * KV_W],
                                _heads_to_slot_major(wb[:, Q_W + 2 * KV_W:], 1)], axis=1).astype(BF16)
    w_out_b_b = _heads_to_slot_major(w_out_b[0], 0).astype(BF16)
    sinks_p = sinks[0].reshape(N_KV, GROUP).T.reshape(1, N_HEADS)

    h1_p, hl_p, tail_p = _prompt_l0(x_prompt, p_prompt, vec0, w_in_a_b, wg, w_out_a_b, wpg[0], wpp[0])
    y_prompt, kl_p, vl_p = _prompt_l1(sinks_p, h1_p, p_prompt, vec1, w_in_b_b, w_out_b_b, wpg[1], wpp[1])
    nbp = x_prompt.shape[0]
    new_h_p = hl_p[:, SUBLANES - 1][None]
    new_conv_p = tail_p[:, SUBLANES - (CONV_W - 1):][None]
    new_k_p = kl_p.reshape(1, nbp, WINDOW, N_KV, HEAD_DIM)
    new_v_p = vl_p.reshape(1, nbp, WINDOW, N_KV, HEAD_DIM)

    rows = steps * nseq
    x_tm = jnp.swapaxes(x_sample, 0, 1).reshape(rows, D_MODEL)
    p_tm = jnp.swapaxes(p_sample, 1, 2).reshape(2, rows, D_PLE)
    cst_tm = jnp.swapaxes(state_conv[0], 0, 1)
    h1_s, hl_s, cnew_tm, q_s, k_s, v_s, gate_s = _sample_l0(
        x_tm, p_tm[0], state_lru_h[0], cst_tm, vec0, vec1, w_in_a_b, wg, w_out_a_b, wpg[0], wpp[0], w_in_b_b)

    sb = SAMPLE_SEQS
    q16 = q_s.reshape(steps, nseq, GROUP, KV_W).transpose(1, 2, 0, 3).reshape(nseq, GROUP * steps, KV_W)

    def new_rows_t(a):
        a = a.reshape(steps, nseq // sb, sb, KV_W).transpose(1, 3, 2, 0).reshape(nseq // sb, KV_W, sb * steps)
        return jnp.concatenate([a, jnp.zeros((nseq // sb, KV_W, LANES - sb * steps), a.dtype)], axis=-1)

    kc = cache_k[0].transpose(0, 2, 3, 1).reshape(nseq, KV_W, WINDOW)
    vc = cache_v[0].transpose(0, 2, 3, 1).reshape(nseq, KV_W, WINDOW)
    head = jnp.arange(N_KV * GROUP * steps) // steps
    slopes = jnp.exp2(-8.0 * (head + 1).astype(F32) / N_HEADS)
    tq = (jnp.arange(N_KV * GROUP * steps) % steps).astype(F32)
    rowinfo = jnp.stack([sinks[0][head], slopes, tq], axis=1)
    rowinfo = jnp.concatenate([rowinfo, jnp.zeros((rowinfo.shape[0], LANES - 3), F32)], axis=1)
    o16, nk, nv = _sample_attn(q16, new_rows_t(k_s), new_rows_t(v_s), kc, vc, rowinfo)

    o4 = o16.reshape(nseq, GROUP, steps, KV_W).transpose(1, 2, 0, 3).reshape(GROUP, rows, KV_W)
    y_tm = _sample_tail(o4, gate_s, h1_s, p_tm[1], vec1, w_out_b_b, wpg[1], wpp[1])
    y_sample = jnp.swapaxes(y_tm.reshape(steps, nseq, D_MODEL), 0, 1)
    new_conv_s = jnp.swapaxes(cnew_tm, 0, 1)[None]
    new_k_s = nk.reshape(nseq, N_KV, HEAD_DIM, WINDOW).transpose(0, 3, 1, 2)[None]
    new_v_s = nv.reshape(nseq, N_KV, HEAD_DIM, WINDOW).transpose(0, 3, 1, 2)[None]

    return (y_prompt, y_sample, new_h_p, new_conv_p, new_k_p, new_v_p,
            hl_s[None], new_conv_s, new_k_s, new_v_s)
```

```python
import functools

import jax
import jax.numpy as jnp
from jax import lax
from jax.experimental import pallas as pl
from jax.experimental.pallas import tpu as pltpu

D_MODEL = 1024
D_RNN = 1024
D_PLE = 256
N_LRU_BLOCKS = 8
LRU_BLOCK = 128
CONV_W = 4
LRU_C = 8.0
EPS = 1e-6
HEAD_DIM = 64
N_HEADS = 16
N_KV = 4
GROUP = 4
Q_W = 1024
KV_W = 256
WINDOW = 128
BLOCK = 128
NEG_INF = -1e30
QK_SCALE = HEAD_DIM ** -0.5
SQRT_FLOOR = 1e-37

LANES = 128
SUBLANES = 8
VMEM_LIMIT = 56 * 1024 * 1024

PROMPT_TILE = 256
SAMPLE_SEQS = 8
ATTN_LAG = 4

F32 = jnp.float32
BF16 = jnp.bfloat16


def _rms(x, g):
    ms = jnp.mean(x * x, axis=-1, keepdims=True)
    return x * lax.rsqrt(ms + EPS) * g


def _sigmoid(x):
    return 0.5 * jnp.tanh(0.5 * x) + 0.5


def _silu(x):
    return x * _sigmoid(x)


def _dot(a, w):
    return jnp.dot(a.astype(BF16), w, preferred_element_type=F32)


def _softplus(z):
    return jnp.maximum(z, 0.0) + jnp.log1p(jnp.exp(-jnp.abs(z)))


def _lru_gate_logits(xc, wg_ref):
    xcb = xc.astype(BF16)
    zr, zi = [], []
    for n in range(N_LRU_BLOCKS):
        z = jnp.dot(xcb[:, n * LRU_BLOCK:(n + 1) * LRU_BLOCK], wg_ref[n], preferred_element_type=F32)
        zr.append(z[:, :LRU_BLOCK])
        zi.append(z[:, LRU_BLOCK:])
    return jnp.concatenate(zr, axis=1), jnp.concatenate(zi, axis=1)


def _lru_coeffs(xc, logits, b_r, b_i, lam):
    r = _sigmoid(logits[0] + b_r)
    i = _sigmoid(logits[1] + b_i)
    log_a = (-LRU_C * _softplus(-lam)) * r
    a = jnp.exp(log_a)
    y = -jnp.tanh(log_a) * (a * a + 1.0)
    b = (y * lax.rsqrt(jnp.maximum(y, SQRT_FLOOR))) * (i * xc)
    return a, b


def _ple(h, p, g, wpg_ref, wpp_ref):
    gate = _sigmoid(_dot(_rms(h, g), wpg_ref[...]))
    return h + gate * _dot(p, wpp_ref[...])


CHUNK = 4
HDR = SUBLANES


def _local_scans(aq, bq):
    big_a, big_b = [aq[0]], [bq[0]]
    for q in range(1, CHUNK):
        big_a.append(aq[q] * big_a[q - 1])
        big_b.append(aq[q] * big_b[q - 1] + bq[q])
    return big_a, big_b


def _carry_in(i, a_tot, b_tot, h0, levels):
    m = a_tot.shape[0]
    if m == CHUNK:
        h, cin = h0, []
        for r in range(CHUNK):
            cin.append(h)
            h = a_tot[r:r + 1] * h + b_tot[r:r + 1]
        return jnp.concatenate(cin, axis=0), h
    a_buf, b_buf, h_buf = levels[0]
    a_buf[i] = a_tot
    b_buf[i] = b_tot
    mq = m // CHUNK
    aq = [a_buf[i, pl.ds(q, mq, stride=CHUNK), :] for q in range(CHUNK)]
    bq = [b_buf[i, pl.ds(q, mq, stride=CHUNK), :] for q in range(CHUNK)]
    big_a, big_b = _local_scans(aq, bq)
    cin, h_last = _carry_in(i, big_a[-1], big_b[-1], h0, levels[1:])
    h_buf[i, HDR - 1:HDR, :] = h0
    for q in range(CHUNK):
        h_buf[i, pl.ds(HDR + q, mq, stride=CHUNK), :] = big_b[q] + big_a[q] * cin
    return h_buf[i, HDR - 1:HDR - 1 + m, :], h_last


V0_NORM, V0_PLE, V0_CB, V0_BR, V0_BI, V0_LAM, V0_CW = 0, 1, 2, 3, 4, 5, 6
V1_NORM, V1_PLE, V1_FINAL = 0, 1, 2


def _prompt_l0_kernel(x_ref, p_ref, vec_ref, win_ref, wg_ref, wout_ref, wpg_ref, wpp_ref,
                      h1_ref, hl_ref, tail_ref,
                      hcar, xb_s, xc_s, zr_s, zi_s, h_s, l2a, l2b, l2h, l3a, l3b, l3h):
    t_rows = x_ref.shape[1]
    nslab = D_RNN // LANES
    levels = ((l2a, l2b, l2h), (l3a, l3b, l3h))

    @pl.when(pl.program_id(0) == 0)
    def _():
        hcar[...] = jnp.zeros_like(hcar)
        xb_s[:, 0:HDR, :] = jnp.zeros((xb_s.shape[0], HDR, LANES), F32)

    st = [dict() for _ in range(x_ref.shape[0])]

    def in_proj(n):
        st[n]["xg"] = _dot(_rms(x_ref[n], vec_ref[V0_NORM:V0_NORM + 1]), win_ref[...])
        st[n]["pp"] = _dot(p_ref[n], wpp_ref[...])

    def conv_gates(n):
        xg = st[n]["xg"]
        tails = []
        for l in range(nslab):
            i = n * nslab + l
            lanes = slice(l * LANES, (l + 1) * LANES)
            xb = xg[:, lanes]
            xb_s[i, HDR:HDR + t_rows, :] = xb
            acc = vec_ref[V0_CB:V0_CB + 1, lanes] + vec_ref[V0_CW + 3:V0_CW + 4, lanes] * xb
            for k in range(1, CONV_W):
                acc = acc + vec_ref[V0_CW + 3 - k:V0_CW + 4 - k, lanes] * xb_s[i, HDR - k:HDR - k + t_rows, :]
            xc_s[i] = acc
            z = jnp.dot(acc.astype(BF16), wg_ref[l], preferred_element_type=F32)
            zr_s[i] = z[:, :LRU_BLOCK]
            zi_s[i] = z[:, LRU_BLOCK:]
            tail = xb[t_rows - HDR:]
            xb_s[i, 0:HDR, :] = tail
            tails.append(tail)
        tail_ref[n] = jnp.concatenate(tails, axis=1)

    def recurrence(n):
        nchunk = t_rows // CHUNK
        for l in range(nslab):
            i = n * nslab + l
            lanes = slice(l * LANES, (l + 1) * LANES)
            aq, bq = [], []
            for q in range(CHUNK):
                rows = pl.ds(q, nchunk, stride=CHUNK)
                a, b = _lru_coeffs(xc_s[i, rows, :], (zr_s[i, rows, :], zi_s[i, rows, :]),
                                   vec_ref[V0_BR:V0_BR + 1, lanes], vec_ref[V0_BI:V0_BI + 1, lanes],
                                   vec_ref[V0_LAM:V0_LAM + 1, lanes])
                aq.append(a)
                bq.append(b)
            big_a, big_b = _local_scans(aq, bq)
            cin, h_last = _carry_in(i, big_a[-1], big_b[-1], hcar[n, :, lanes], levels)
            for q in range(CHUNK):
                h_s[i, pl.ds(q, nchunk, stride=CHUNK), :] = big_b[q] + big_a[q] * cin
            hcar[n, :, lanes] = h_last
        h = jnp.concatenate([h_s[n * nslab + l] for l in range(nslab)], axis=1)
        hl_ref[n] = h[t_rows - SUBLANES:]
        st[n]["hg"] = h * _silu(st[n]["xg"][:, D_RNN:])

    def out_proj(n):
        st[n]["h1"] = x_ref[n] + _dot(st[n]["hg"], wout_ref[...])

    def ple(n):
        h1 = st[n]["h1"]
        gl = _dot(_rms(h1, vec_ref[V0_PLE:V0_PLE + 1]), wpg_ref[...])
        h1_ref[n] = h1 + _sigmoid(gl) * st[n]["pp"]

    for stage in (in_proj, conv_gates):
        for n in range(len(st)):
            stage(n)
    for n in range(len(st)):
        recurrence(n)
        out_proj(n)
    for n in range(len(st)):
        ple(n)


def _const_spec(shape):
    return pl.BlockSpec(shape, lambda *_: (0,) * len(shape), pipeline_mode=pl.Buffered(1))


def _prompt_l0(x, p_all, vec0, w_in_a, wg, w_out_a, wpg0, wpp0):
    nb, seq, _ = x.shape
    t = PROMPT_TILE
    assert t % CHUNK ** 3 == 0 and t // CHUNK ** 3 == CHUNK, "three strided levels, then a CHUNK-row serial scan"
    nslab = nb * (D_RNN // LANES)
    return pl.pallas_call(
        _prompt_l0_kernel,
        grid=(seq // t,),
        in_specs=[
            pl.BlockSpec((nb, t, D_MODEL), lambda s: (0, s, 0)),
            pl.BlockSpec((None, nb, t, D_PLE), lambda s: (0, 0, s, 0)),
            _const_spec(vec0.shape), _const_spec(w_in_a.shape), _const_spec(wg.shape),
            _const_spec(w_out_a.shape), _const_spec(wpg0.shape), _const_spec(wpp0.shape),
        ],
        out_specs=[
            pl.BlockSpec((nb, t, D_MODEL), lambda s: (0, s, 0)),
            pl.BlockSpec((nb, SUBLANES, D_RNN), lambda s: (0, 0, 0)),
            pl.BlockSpec((nb, SUBLANES, D_RNN), lambda s: (0, 0, 0)),
        ],
        out_shape=[
            jax.ShapeDtypeStruct((nb, seq, D_MODEL), F32),
            jax.ShapeDtypeStruct((nb, SUBLANES, D_RNN), F32),
            jax.ShapeDtypeStruct((nb, SUBLANES, D_RNN), F32),
        ],
        scratch_shapes=[pltpu.VMEM((nb, 1, D_RNN), F32), pltpu.VMEM((nslab, HDR + t, LANES), F32)]
        + [pltpu.VMEM((nslab, t, LANES), F32)] * 4
        + [pltpu.VMEM((nslab, rows, LANES), F32)
           for m in (t // CHUNK, t // CHUNK ** 2) for rows in (m, m, HDR + m)],
        compiler_params=pltpu.CompilerParams(
            dimension_semantics=("arbitrary",), vmem_limit_bytes=VMEM_LIMIT),
        name="prompt_l0",
    )(x, p_all, vec0, w_in_a, wg, w_out_a, wpg0, wpp0)


def _slope(head):
    return 2.0 ** (-8.0 * (head + 1) / N_HEADS)


def _build_prompt_bias(bias_ref):
    shape = (BLOCK, 4 * BLOCK)
    qi = lax.broadcasted_iota(jnp.int32, shape, 0)
    col = lax.broadcasted_iota(jnp.int32, shape, 1)
    si = col & (2 * BLOCK - 1)
    second = col >= 2 * BLOCK
    dist = BLOCK + qi - si
    valid = (dist >= 0) & (dist < WINDOW)
    valid_first = valid & (si >= BLOCK)
    distf = dist.astype(F32)
    for hh in range(GROUP):
        for i in range(N_KV // 2):
            slope = jnp.where(second, _slope(GROUP * (2 * i + 1) + hh), _slope(GROUP * (2 * i) + hh))
            b = -(slope * distf)
            bias_ref[0, hh * 2 + i] = jnp.where(valid_first, b, NEG_INF)
            bias_ref[1, hh * 2 + i] = jnp.where(valid, b, NEG_INF)


def _masked_pair(x2, lo):
    return jnp.concatenate([jnp.where(lo, x2, 0.0), jnp.where(lo, 0.0, x2)], axis=0).astype(BF16)


def _pair_scores(q2, k_both, bias):
    return lax.dot_general(q2, k_both, (((1,), (1,)), ((), ())), preferred_element_type=F32) + bias


def _pair_values(s2, v_both, sinks, lo):
    es, inv = [], []
    for half in range(2):
        sh = s2[:, half * 2 * BLOCK:(half + 1) * 2 * BLOCK]
        m = jnp.maximum(jnp.max(sh, axis=-1, keepdims=True), sinks[half])
        e = jnp.exp(sh - m)
        denom = jnp.sum(e, axis=-1, keepdims=True) + jnp.exp(sinks[half] - m)
        es.append(e.astype(BF16))
        inv.append(1.0 / denom)
    o2 = jnp.dot(jnp.concatenate(es, axis=1), v_both, preferred_element_type=F32)
    return o2 * jnp.where(lo, inv[0], inv[1])


PROJ_W = Q_W + 2 * KV_W + Q_W
NCOL = 256


def _spread(n_tasks, n_slots):
    return [(s + 1) * n_tasks // n_slots - s * n_tasks // n_slots for s in range(n_slots)]


def _prompt_l1_kernel(sink_ref, h1_ref, h1n_ref, p_ref, vec_ref, win_ref, wout_ref, wpg_ref, wpp_ref,
                      y_ref, kl_ref, vl_ref, kprev, vprev, bias_ref, proj_s):
    nstreams, t_rows = h1_ref.shape[0], h1_ref.shape[1]
    assert nstreams == 2
    nblk = t_rows // BLOCK
    nchunk = PROJ_W // NCOL
    s_idx = pl.program_id(0)

    def proj_tasks(n, src):
        state = {}

        def norm():
            state["u"] = _rms(src(), vec_ref[V1_NORM:V1_NORM + 1]).astype(BF16)

        def chunk(c):
            proj_s[n, c] = jnp.dot(state["u"], win_ref[:, c * NCOL:(c + 1) * NCOL], preferred_element_type=F32)

        return [norm] + [functools.partial(chunk, c) for c in range(nchunk)]

    @pl.when(s_idx == 0)
    def _():
        _build_prompt_bias(bias_ref)
        kprev[...] = jnp.zeros_like(kprev)
        vprev[...] = jnp.zeros_like(vprev)
        for task in proj_tasks(0, lambda: h1_ref[0]):
            task()

    lo = lax.broadcasted_iota(jnp.int32, (1, LANES), 1) < HEAD_DIM
    first_tbl = jnp.where(s_idx == 0, 0, 1)
    streams = range(nstreams)

    def proj(n, rows, col0, width):
        c, off = divmod(col0, NCOL)
        assert off + width <= NCOL
        return proj_s[n, c, rows, off:off + width]

    def kv_block(n, j, base, carry_ref):
        if j < 0:
            return carry_ref[n]
        return proj(n, slice(j * BLOCK, (j + 1) * BLOCK), base, KV_W)

    cache = {}

    def masked_kv(n, j, i, base, carry_ref):
        key = (n, j, i, base)
        if key not in cache:
            lanes = slice(i * LANES, (i + 1) * LANES)
            prev, cur = kv_block(n, j - 1, base, carry_ref), kv_block(n, j, base, carry_ref)
            cache[key] = _masked_pair(jnp.concatenate([prev[:, lanes], cur[:, lanes]], axis=0), lo)
        return cache[key]

    def scores_of(n, j, i, hh):
        q2 = proj(n, slice(j * BLOCK, (j + 1) * BLOCK), hh * KV_W + i * LANES, LANES)
        q2 = (q2 * QK_SCALE).astype(BF16)
        return _pair_scores(q2, masked_kv(n, j, i, Q_W, kprev), bias_ref[first_tbl if j == 0 else 1, hh * 2 + i])

    def values_of(n, j, i, hh, s2):
        sinks = [sink_ref[0, hh * N_KV + 2 * i + half] for half in range(2)]
        return _pair_values(s2, masked_kv(n, j, i, Q_W + KV_W, vprev), sinks, lo)

    pp = {}
    o_blocks = {n: [] for n in streams}

    def ple_proj_task(n):
        def run():
            pp[n] = _dot(p_ref[n], wpp_ref[...])
        return run

    def tail_tasks(n):
        state = {}
        ncol = D_MODEL // NCOL

        def start():
            k_last = kv_block(n, nblk - 1, Q_W, kprev)
            v_last = kv_block(n, nblk - 1, Q_W + KV_W, vprev)
            kprev[n] = k_last
            vprev[n] = v_last
            kl_ref[n] = k_last
            vl_ref[n] = v_last
            gate = jnp.concatenate([proj_s[n, c] for c in range((Q_W + 2 * KV_W) // NCOL, nchunk)], axis=1)
            state["hg"] = (jnp.concatenate(o_blocks[n], axis=0) * _silu(gate)).astype(BF16)

        def out_chunk(c):
            cols = slice(c * NCOL, (c + 1) * NCOL)
            state["h2", c] = h1_ref[n, :, cols] + jnp.dot(state["hg"], wout_ref[:, cols], preferred_element_type=F32)

        def norm():
            state["h2"] = jnp.concatenate([state["h2", c] for c in range(ncol)], axis=1)
            state["nrm"] = _rms(state["h2"], vec_ref[V1_PLE:V1_PLE + 1]).astype(BF16)

        def gate_chunk(c):
            cols = slice(c * NCOL, (c + 1) * NCOL)
            gl = jnp.dot(state["nrm"], wpg_ref[:, cols], preferred_element_type=F32)
            state["out", c] = state["h2"][:, cols] + _sigmoid(gl) * pp[n][:, cols]

        def final():
            out = jnp.concatenate([state["out", c] for c in range(ncol)], axis=1)
            y_ref[n] = _rms(out, vec_ref[V1_FINAL:V1_FINAL + 1])

        return ([start] + [functools.partial(out_chunk, c) for c in range(ncol)] + [norm]
                + [functools.partial(gate_chunk, c) for c in range(ncol)] + [final])

    units = [(n, j, i, hh) for n in streams for j in range(nblk) for i in range(N_KV // 2) for hh in range(GROUP)]
    per_stream = len(units) // nstreams
    per_block = (N_KV // 2) * GROUP
    fill_first = proj_tasks(1, lambda: h1_ref[1]) + [ple_proj_task(0), ple_proj_task(1)]
    fill_second = tail_tasks(0)
    plan = ([fill_first.pop(0) for _ in range(k)] for k in _spread(len(fill_first), per_stream))
    plan = list(plan) + [[] for _ in range(ATTN_LAG)]
    plan += [[fill_second.pop(0) for _ in range(k)] for k in _spread(len(fill_second), per_stream)]
    scores, cols = {}, {}
    for t in range(len(units) + ATTN_LAG):
        if t < len(units):
            scores[t] = scores_of(*units[t])
        if t >= ATTN_LAG:
            n, j, i, hh = units[t - ATTN_LAG]
            cols[hh * 2 + i] = values_of(n, j, i, hh, scores.pop(t - ATTN_LAG))
            if len(cols) == per_block:
                o_blocks[n].append(jnp.concatenate([cols[c] for c in range(per_block)], axis=1))
                cols = {}
        for task in plan[t]:
            task()

    ahead = proj_tasks(0, lambda: h1n_ref[...])
    for pair in zip(ahead, tail_tasks(1)):
        for task in pair:
            task()


def _prompt_l1(sinks_p, h1, p_all, vec1, w_in_b, w_out_b, wpg1, wpp1):
    nb, seq, _ = h1.shape
    t = PROMPT_TILE
    return pl.pallas_call(
        _prompt_l1_kernel,
        grid=(seq // t,),
        in_specs=[
            pl.BlockSpec(memory_space=pltpu.SMEM),
            pl.BlockSpec((nb, t, D_MODEL), lambda s: (0, s, 0)),
            pl.BlockSpec((None, t, D_MODEL), lambda s: (0, jnp.minimum(s + 1, seq // t - 1), 0)),
            pl.BlockSpec((None, nb, t, D_PLE), lambda s: (1, 0, s, 0)),
            _const_spec(vec1.shape), _const_spec(w_in_b.shape), _const_spec(w_out_b.shape),
            _const_spec(wpg1.shape), _const_spec(wpp1.shape),
        ],
        out_specs=[
            pl.BlockSpec((nb, t, D_MODEL), lambda s: (0, s, 0)),
            pl.BlockSpec((nb, BLOCK, KV_W), lambda s: (0, 0, 0)),
            pl.BlockSpec((nb, BLOCK, KV_W), lambda s: (0, 0, 0)),
        ],
        out_shape=[
            jax.ShapeDtypeStruct((nb, seq, D_MODEL), F32),
            jax.ShapeDtypeStruct((nb, BLOCK, KV_W), F32),
            jax.ShapeDtypeStruct((nb, BLOCK, KV_W), F32),
        ],
        scratch_shapes=[
            pltpu.VMEM((nb, BLOCK, KV_W), F32), pltpu.VMEM((nb, BLOCK, KV_W), F32),
            pltpu.VMEM((2, 2 * GROUP, BLOCK, 4 * BLOCK), F32),
            pltpu.VMEM((nb, PROJ_W // NCOL, t, NCOL), F32),
        ],
        compiler_params=pltpu.CompilerParams(
            dimension_semantics=("arbitrary",), vmem_limit_bytes=VMEM_LIMIT),
        name="prompt_l1",
    )(sinks_p, h1, h1, p_all, vec1, w_in_b, w_out_b, wpg1, wpp1)


def _sample_l0_kernel(x_ref, p_ref, h0_ref, cst_ref, vec_ref, vec1_ref, win_ref, wg_ref, wout_ref, wpg_ref, wpp_ref,
                      winb_ref, h1_ref, hl_ref, cnew_ref, q_ref, k_ref, v_ref, gate_ref):
    nseq = h0_ref.shape[0]
    steps = x_ref.shape[0] // nseq
    x = x_ref[...]
    xg = _dot(_rms(x, vec_ref[V0_NORM:V0_NORM + 1]), win_ref[...])
    xb = xg[:, :D_RNN]
    gate = xg[:, D_RNN:]
    hist = [cst_ref[k] for k in range(CONV_W - 1)] + [xb[t * nseq:(t + 1) * nseq] for t in range(steps)]
    xc = []
    for t in range(steps):
        acc = vec_ref[V0_CB:V0_CB + 1] + vec_ref[V0_CW:V0_CW + 1] * hist[t]
        for k in range(1, CONV_W):
            acc = acc + vec_ref[V0_CW + k:V0_CW + k + 1] * hist[t + k]
        xc.append(acc)
    for k in range(CONV_W - 1):
        cnew_ref[k] = hist[steps + k]
    xc = jnp.concatenate(xc, axis=0)
    a, b = _lru_coeffs(xc, _lru_gate_logits(xc, wg_ref), vec_ref[V0_BR:V0_BR + 1], vec_ref[V0_BI:V0_BI + 1],
                       vec_ref[V0_LAM:V0_LAM + 1])
    h = h0_ref[...]
    hs = []
    for t in range(steps):
        h = a[t * nseq:(t + 1) * nseq] * h + b[t * nseq:(t + 1) * nseq]
        hs.append(h)
    hl_ref[...] = h
    h1 = x + _dot(jnp.concatenate(hs, axis=0) * _silu(gate), wout_ref[...])
    h1 = _ple(h1, p_ref[...], vec_ref[V0_PLE:V0_PLE + 1], wpg_ref, wpp_ref)
    h1_ref[...] = h1
    proj = _dot(_rms(h1, vec1_ref[V1_NORM:V1_NORM + 1]), winb_ref[...])
    q_ref[...] = proj[:, :Q_W] * QK_SCALE
    k_ref[...] = proj[:, Q_W:Q_W + KV_W]
    v_ref[...] = proj[:, Q_W + KV_W:Q_W + 2 * KV_W]
    gate_ref[...] = proj[:, Q_W + 2 * KV_W:]


def _sample_l0(x_tm, p0_tm, h0, cst_tm, vec0, vec1, w_in_a, wg, w_out_a, wpg0, wpp0, w_in_b):
    rows = x_tm.shape[0]
    nseq = h0.shape[0]
    args = (x_tm, p0_tm, h0, cst_tm, vec0, vec1, w_in_a, wg, w_out_a, wpg0, wpp0, w_in_b)
    out_shapes = [(rows, D_MODEL), (nseq, D_RNN), (CONV_W - 1, nseq, D_RNN),
                  (rows, Q_W), (rows, KV_W), (rows, KV_W), (rows, Q_W)]
    return pl.pallas_call(
        _sample_l0_kernel,
        grid=(1,),
        in_specs=[_const_spec(a.shape) for a in args],
        out_specs=[pl.BlockSpec(s, lambda *_, n=len(s): (0,) * n) for s in out_shapes],
        out_shape=[jax.ShapeDtypeStruct(s, F32) for s in out_shapes],
        compiler_params=pltpu.CompilerParams(dimension_semantics=("arbitrary",), vmem_limit_bytes=VMEM_LIMIT),
        name="sample_l0",
    )(*args)


def _sample_attn_kernel(q_ref, knt_ref, vnt_ref, kc_ref, vc_ref, rowinfo_ref, o_ref, nk_ref, nv_ref):
    nseq = q_ref.shape[0]
    nq = q_ref.shape[1]
    steps = nq // GROUP
    rows = N_KV * nq
    sink = rowinfo_ref[:, 0:1]
    slope = rowinfo_ref[:, 1:2]
    tq = rowinfo_ref[:, 2:3]
    col = lax.broadcasted_iota(jnp.int32, (rows, 2 * WINDOW), 1).astype(F32)
    dist = WINDOW + tq - col
    valid = (dist >= 0.0) & (dist < WINDOW)
    bias = jnp.where(valid, -(slope * dist), NEG_INF)
    lane_grp = lax.broadcasted_iota(jnp.int32, (1, KV_W), 1) // HEAD_DIM
    new_lane = lax.broadcasted_iota(jnp.int32, (1, WINDOW), 1) < steps
    knt_all = knt_ref[...]
    vnt_all = vnt_ref[...]

    for bl in range(nseq):
        q = q_ref[bl]
        q_full = jnp.concatenate([jnp.where(lane_grp == g, q, 0.0) for g in range(N_KV)], axis=0).astype(BF16)
        shift = (LANES - steps * bl) % LANES
        knt = knt_all if shift == 0 else pltpu.roll(knt_all, shift, 1)
        vnt = vnt_all if shift == 0 else pltpu.roll(vnt_all, shift, 1)
        kt_ext = jnp.concatenate([kc_ref[bl], jnp.where(new_lane, knt, 0.0)], axis=1)
        vt_ext = jnp.concatenate([vc_ref[bl], jnp.where(new_lane, vnt, 0.0)], axis=1)
        s = jnp.dot(q_full, kt_ext.astype(BF16), preferred_element_type=F32) + bias
        m = jnp.maximum(jnp.max(s, axis=-1, keepdims=True), sink)
        e = jnp.exp(s - m)
        denom = jnp.sum(e, axis=-1, keepdims=True) + jnp.exp(sink - m)
        o = lax.dot_general(e.astype(BF16), vt_ext.astype(BF16), (((1,), (1,)), ((), ())),
                            preferred_element_type=F32) * (1.0 / denom)
        acc = jnp.where(lane_grp == 0, o[:nq], 0.0)
        for g in range(1, N_KV):
            acc = acc + jnp.where(lane_grp == g, o[g * nq:(g + 1) * nq], 0.0)
        o_ref[bl] = acc
        nk_ref[bl] = pltpu.roll(kt_ext, 2 * WINDOW - steps, 1)[:, :WINDOW]
        nv_ref[bl] = pltpu.roll(vt_ext, 2 * WINDOW - steps, 1)[:, :WINDOW]


def _sample_attn(q16, knt, vnt, kc, vc, rowinfo):
    nseq_total, nq, _ = q16.shape
    sb = SAMPLE_SEQS
    blk3 = lambda i: (i, 0, 0)
    return pl.pallas_call(
        _sample_attn_kernel,
        grid=(nseq_total // sb,),
        in_specs=[
            pl.BlockSpec((sb, nq, KV_W), blk3),
            pl.BlockSpec((None, KV_W, LANES), blk3),
            pl.BlockSpec((None, KV_W, LANES), blk3),
            pl.BlockSpec((sb, KV_W, WINDOW), blk3),
            pl.BlockSpec((sb, KV_W, WINDOW), blk3),
            pl.BlockSpec(rowinfo.shape, lambda i: (0, 0)),
        ],
        out_specs=[
            pl.BlockSpec((sb, nq, KV_W), blk3),
            pl.BlockSpec((sb, KV_W, WINDOW), blk3),
            pl.BlockSpec((sb, KV_W, WINDOW), blk3),
        ],
        out_shape=[
            jax.ShapeDtypeStruct((nseq_total, nq, KV_W), F32),
            jax.ShapeDtypeStruct((nseq_total, KV_W, WINDOW), F32),
            jax.ShapeDtypeStruct((nseq_total, KV_W, WINDOW), F32),
        ],
        compiler_params=pltpu.CompilerParams(dimension_semantics=("arbitrary",), vmem_limit_bytes=VMEM_LIMIT),
        name="sample_attn",
    )(q16, knt, vnt, kc, vc, rowinfo)


def _sample_tail_kernel(o_ref, gate_ref, h1_ref, p_ref, vec_ref, wout_ref, wpg_ref, wpp_ref, y_ref):
    o = jnp.concatenate([o_ref[hh] for hh in range(GROUP)], axis=1)
    h1 = h1_ref[...]
    h2 = h1 + _dot(o * _silu(gate_ref[...]), wout_ref[...])
    h2 = _ple(h2, p_ref[...], vec_ref[V1_PLE:V1_PLE + 1], wpg_ref, wpp_ref)
    y_ref[...] = _rms(h2, vec_ref[V1_FINAL:V1_FINAL + 1])


def _sample_tail(o4, gate, h1, p1_tm, vec1, w_out_b, wpg1, wpp1):
    args = (o4, gate, h1, p1_tm, vec1, w_out_b, wpg1, wpp1)
    return pl.pallas_call(
        _sample_tail_kernel,
        grid=(1,),
        in_specs=[_const_spec(a.shape) for a in args],
        out_specs=pl.BlockSpec(h1.shape, lambda i: (0, 0)),
        out_shape=jax.ShapeDtypeStruct(h1.shape, F32),
        compiler_params=pltpu.CompilerParams(dimension_semantics=("arbitrary",), vmem_limit_bytes=VMEM_LIMIT),
        name="sample_tail",
    )(*args)


def _heads_to_slot_major(w, axis):
    shape = w.shape
    w = w.reshape(shape[:axis] + (N_KV, GROUP, HEAD_DIM) + shape[axis + 1:])
    w = jnp.swapaxes(w, axis, axis + 1)
    return w.reshape(shape)


def _pad_rows(v, rows):
    return jnp.concatenate([v, jnp.zeros((rows - v.shape[0], v.shape[1]), v.dtype)], axis=0)


def kernel(x_prompt, x_sample, p_prompt, p_sample, state_lru_h, state_conv, cache_k, cache_v, norm_g, final_norm_g, ple_norm_g, w_ple_gate, w_ple_proj, w_in_a, conv_w_a, conv_b_a, w_rgate, b_rgate, w_igate, b_igate, lru_lambda, w_out_a, w_in_b, sinks, w_out_b):
    assert norm_g.shape[0] == 2 and w_in_a.shape[0] == 1 and w_in_b.shape[0] == 1
    nseq, steps, _ = x_sample.shape

    vec0 = _pad_rows(jnp.concatenate([norm_g[0:1], ple_norm_g[0:1], conv_b_a, b_rgate, b_igate, lru_lambda,
                                      conv_w_a[0]], axis=0), 2 * SUBLANES)
    vec1 = _pad_rows(jnp.concatenate([norm_g[1:2], ple_norm_g[1:2], final_norm_g[None]], axis=0), SUBLANES)
    w_in_a_b = w_in_a[0].astype(BF16)
    w_out_a_b = w_out_a[0].astype(BF16)
    wg = jnp.concatenate([w_rgate[0], w_igate[0]], axis=-1).astype(BF16)
    wpg = w_ple_gate.astype(BF16)
    wpp = w_ple_proj.astype(BF16)
    wb = w_in_b[0]
    w_in_b_b = jnp.concatenate([_heads_to_slot_major(wb[:, :Q_W], 1), wb[:, Q_W:Q_W + 2 * KV_W],
                                _heads_to_slot_major(wb[:, Q_W + 2 * KV_W:], 1)], axis=1).astype(BF16)
    w_out_b_b = _heads_to_slot_major(w_out_b[0], 0).astype(BF16)
    sinks_p = sinks[0].reshape(N_KV, GROUP).T.reshape(1, N_HEADS)

    h1_p, hl_p, tail_p = _prompt_l0(x_prompt, p_prompt, vec0, w_in_a_b, wg, w_out_a_b, wpg[0], wpp[0])
    y_prompt, kl_p, vl_p = _prompt_l1(sinks_p, h1_p, p_prompt, vec1, w_in_b_b, w_out_b_b, wpg[1], wpp[1])
    nbp = x_prompt.shape[0]
    new_h_p = hl_p[:, SUBLANES - 1][None]
    new_conv_p = tail_p[:, SUBLANES - (CONV_W - 1):][None]
    new_k_p = kl_p.reshape(1, nbp, WINDOW, N_KV, HEAD_DIM)
    new_v_p = vl_p.reshape(1, nbp, WINDOW, N_KV, HEAD_DIM)

    rows = steps * nseq
    x_tm = jnp.swapaxes(x_sample, 0, 1).reshape(rows, D_MODEL)
    p_tm = jnp.swapaxes(p_sample, 1, 2).reshape(2, rows, D_PLE)
    cst_tm = jnp.swapaxes(state_conv[0], 0, 1)
    h1_s, hl_s, cnew_tm, q_s, k_s, v_s, gate_s = _sample_l0(
        x_tm, p_tm[0], state_lru_h[0], cst_tm, vec0, vec1, w_in_a_b, wg, w_out_a_b, wpg[0], wpp[0], w_in_b_b)

    sb = SAMPLE_SEQS
    q16 = q_s.reshape(steps, nseq, GROUP, KV_W).transpose(1, 2, 0, 3).reshape(nseq, GROUP * steps, KV_W)

    def new_rows_t(a):
        a = a.reshape(steps, nseq // sb, sb, KV_W).transpose(1, 3, 2, 0).reshape(nseq // sb, KV_W, sb * steps)
        return jnp.concatenate([a, jnp.zeros((nseq // sb, KV_W, LANES - sb * steps), a.dtype)], axis=-1)

    kc = cache_k[0].transpose(0, 2, 3, 1).reshape(nseq, KV_W, WINDOW)
    vc = cache_v[0].transpose(0, 2, 3, 1).reshape(nseq, KV_W, WINDOW)
    head = jnp.arange(N_KV * GROUP * steps) // steps
    slopes = jnp.exp2(-8.0 * (head + 1).astype(F32) / N_HEADS)
    tq = (jnp.arange(N_KV * GROUP * steps) % steps).astype(F32)
    rowinfo = jnp.stack([sinks[0][head], slopes, tq], axis=1)
    rowinfo = jnp.concatenate([rowinfo, jnp.zeros((rowinfo.shape[0], LANES - 3), F32)], axis=1)
    o16, nk, nv = _sample_attn(q16, new_rows_t(k_s), new_rows_t(v_s), kc, vc, rowinfo)

    o4 = o16.reshape(nseq, GROUP, steps, KV_W).transpose(1, 2, 0, 3).reshape(GROUP, rows, KV_W)
    y_tm = _sample_tail(o4, gate_s, h1_s, p_tm[1], vec1, w_out_b_b, wpg[1], wpp[1])
    y_sample = jnp.swapaxes(y_tm.reshape(steps, nseq, D_MODEL), 0, 1)
    new_conv_s = jnp.swapaxes(cnew_tm, 0, 1)[None]
    new_k_s = nk.reshape(nseq, N_KV, HEAD_DIM, WINDOW).transpose(0, 3, 1, 2)[None]
    new_v_s = nv.reshape(nseq, N_KV, HEAD_DIM, WINDOW).transpose(0, 3, 1, 2)[None]

    return (y_prompt, y_sample, new_h_p, new_conv_p, new_k_p, new_v_p,
            hl_s[None], new_conv_s, new_k_s, new_v_s)
```

```python
import functools

import jax
import jax.numpy as jnp
from jax import lax
from jax.experimental import pallas as pl
from jax.experimental.pallas import tpu as pltpu

D_MODEL = 1024
D_RNN = 1024
D_PLE = 256
N_LRU_BLOCKS = 8
LRU_BLOCK = 128
CONV_W = 4
LRU_C = 8.0
EPS = 1e-6
HEAD_DIM = 64
N_HEADS = 16
N_KV = 4
GROUP = 4
Q_W = 1024
KV_W = 256
WINDOW = 128
BLOCK = 128
NEG_INF = -1e30
QK_SCALE = HEAD_DIM ** -0.5
SQRT_FLOOR = 1e-37

LANES = 128
SUBLANES = 8
VMEM_LIMIT = 56 * 1024 * 1024

PROMPT_TILE = 256
SAMPLE_SEQS = 8
ATTN_LAG = 4

F32 = jnp.float32
BF16 = jnp.bfloat16


def _rms(x, g):
    ms = jnp.mean(x * x, axis=-1, keepdims=True)
    return x * lax.rsqrt(ms + EPS) * g


def _sigmoid(x):
    return 0.5 * jnp.tanh(0.5 * x) + 0.5


def _silu(x):
    return x * _sigmoid(x)


def _dot(a, w):
    return jnp.dot(a.astype(BF16), w, preferred_element_type=F32)


def _softplus(z):
    return jnp.maximum(z, 0.0) + jnp.log1p(jnp.exp(-jnp.abs(z)))


def _lru_gate_logits(xc, wg_ref):
    xcb = xc.astype(BF16)
    zr, zi = [], []
    for n in range(N_LRU_BLOCKS):
        z = jnp.dot(xcb[:, n * LRU_BLOCK:(n + 1) * LRU_BLOCK], wg_ref[n], preferred_element_type=F32)
        zr.append(z[:, :LRU_BLOCK])
        zi.append(z[:, LRU_BLOCK:])
    return jnp.concatenate(zr, axis=1), jnp.concatenate(zi, axis=1)


def _lru_coeffs(xc, logits, b_r, b_i, lam):
    r = _sigmoid(logits[0] + b_r)
    i = _sigmoid(logits[1] + b_i)
    log_a = (-LRU_C * _softplus(-lam)) * r
    a = jnp.exp(log_a)
    y = -jnp.tanh(log_a) * (a * a + 1.0)
    b = (y * lax.rsqrt(jnp.maximum(y, SQRT_FLOOR))) * (i * xc)
    return a, b


def _ple(h, p, g, wpg_ref, wpp_ref):
    gate = _sigmoid(_dot(_rms(h, g), wpg_ref[...]))
    return h + gate * _dot(p, wpp_ref[...])


CHUNK = 4
HDR = SUBLANES


def _local_scans(aq, bq):
    big_a, big_b = [aq[0]], [bq[0]]
    for q in range(1, CHUNK):
        big_a.append(aq[q] * big_a[q - 1])
        big_b.append(aq[q] * big_b[q - 1] + bq[q])
    return big_a, big_b


def _carry_in(i, a_tot, b_tot, h0, levels):
    m = a_tot.shape[0]
    if m == CHUNK:
        h, cin = h0, []
        for r in range(CHUNK):
            cin.append(h)
            h = a_tot[r:r + 1] * h + b_tot[r:r + 1]
        return jnp.concatenate(cin, axis=0), h
    a_buf, b_buf, h_buf = levels[0]
    a_buf[i] = a_tot
    b_buf[i] = b_tot
    mq = m // CHUNK
    aq = [a_buf[i, pl.ds(q, mq, stride=CHUNK), :] for q in range(CHUNK)]
    bq = [b_buf[i, pl.ds(q, mq, stride=CHUNK), :] for q in range(CHUNK)]
    big_a, big_b = _local_scans(aq, bq)
    cin, h_last = _carry_in(i, big_a[-1], big_b[-1], h0, levels[1:])
    h_buf[i, HDR - 1:HDR, :] = h0
    for q in range(CHUNK):
        h_buf[i, pl.ds(HDR + q, mq, stride=CHUNK), :] = big_b[q] + big_a[q] * cin
    return h_buf[i, HDR - 1:HDR - 1 + m, :], h_last


V0_NORM, V0_PLE, V0_CB, V0_BR, V0_BI, V0_LAM, V0_CW = 0, 1, 2, 3, 4, 5, 6
V1_NORM, V1_PLE, V1_FINAL = 0, 1, 2


def _prompt_l0_kernel(x_ref, p_ref, vec_ref, win_ref, wg_ref, wout_ref, wpg_ref, wpp_ref,
                      h1_ref, hl_ref, tail_ref,
                      hcar, xb_s, xc_s, zr_s, zi_s, h_s, l2a, l2b, l2h, l3a, l3b, l3h):
    t_rows = x_ref.shape[1]
    nslab = D_RNN // LANES
    levels = ((l2a, l2b, l2h), (l3a, l3b, l3h))

    @pl.when(pl.program_id(0) == 0)
    def _():
        hcar[...] = jnp.zeros_like(hcar)
        xb_s[:, 0:HDR, :] = jnp.zeros((xb_s.shape[0], HDR, LANES), F32)

    st = [dict() for _ in range(x_ref.shape[0])]

    def in_proj(n):
        st[n]["xg"] = _dot(_rms(x_ref[n], vec_ref[V0_NORM:V0_NORM + 1]), win_ref[...])
        st[n]["pp"] = _dot(p_ref[n], wpp_ref[...])

    def conv_gates(n):
        xg = st[n]["xg"]
        tails = []
        for l in range(nslab):
            i = n * nslab + l
            lanes = slice(l * LANES, (l + 1) * LANES)
            xb = xg[:, lanes]
            xb_s[i, HDR:HDR + t_rows, :] = xb
            acc = vec_ref[V0_CB:V0_CB + 1, lanes] + vec_ref[V0_CW + 3:V0_CW + 4, lanes] * xb
            for k in range(1, CONV_W):
                acc = acc + vec_ref[V0_CW + 3 - k:V0_CW + 4 - k, lanes] * xb_s[i, HDR - k:HDR - k + t_rows, :]
            xc_s[i] = acc
            z = jnp.dot(acc.astype(BF16), wg_ref[l], preferred_element_type=F32)
            zr_s[i] = z[:, :LRU_BLOCK]
            zi_s[i] = z[:, LRU_BLOCK:]
            tail = xb[t_rows - HDR:]
            xb_s[i, 0:HDR, :] = tail
            tails.append(tail)
        tail_ref[n] = jnp.concatenate(tails, axis=1)

    def recurrence(n):
        nchunk = t_rows // CHUNK
        for l in range(nslab):
            i = n * nslab + l
            lanes = slice(l * LANES, (l + 1) * LANES)
            aq, bq = [], []
            for q in range(CHUNK):
                rows = pl.ds(q, nchunk, stride=CHUNK)
                a, b = _lru_coeffs(xc_s[i, rows, :], (zr_s[i, rows, :], zi_s[i, rows, :]),
                                   vec_ref[V0_BR:V0_BR + 1, lanes], vec_ref[V0_BI:V0_BI + 1, lanes],
                                   vec_ref[V0_LAM:V0_LAM + 1, lanes])
                aq.append(a)
                bq.append(b)
            big_a, big_b = _local_scans(aq, bq)
            cin, h_last = _carry_in(i, big_a[-1], big_b[-1], hcar[n, :, lanes], levels)
            for q in range(CHUNK):
                h_s[i, pl.ds(q, nchunk, stride=CHUNK), :] = big_b[q] + big_a[q] * cin
            hcar[n, :, lanes] = h_last
        h = jnp.concatenate([h_s[n * nslab + l] for l in range(nslab)], axis=1)
        hl_ref[n] = h[t_rows - SUBLANES:]
        st[n]["hg"] = h * _silu(st[n]["xg"][:, D_RNN:])

    def out_proj(n):
        st[n]["h1"] = x_ref[n] + _dot(st[n]["hg"], wout_ref[...])

    def ple(n):
        h1 = st[n]["h1"]
        gl = _dot(_rms(h1, vec_ref[V0_PLE:V0_PLE + 1]), wpg_ref[...])
        h1_ref[n] = h1 + _sigmoid(gl) * st[n]["pp"]

    for stage in (in_proj, conv_gates):
        for n in range(len(st)):
            stage(n)
    for n in range(len(st)):
        recurrence(n)
        out_proj(n)
    for n in range(len(st)):
        ple(n)


def _const_spec(shape):
    return pl.BlockSpec(shape, lambda *_: (0,) * len(shape), pipeline_mode=pl.Buffered(1))


def _prompt_l0(x, p_all, vec0, w_in_a, wg, w_out_a, wpg0, wpp0):
    nb, seq, _ = x.shape
    t = PROMPT_TILE
    assert t % CHUNK ** 3 == 0 and t // CHUNK ** 3 == CHUNK, "three strided levels, then a CHUNK-row serial scan"
    nslab = nb * (D_RNN // LANES)
    return pl.pallas_call(
        _prompt_l0_kernel,
        grid=(seq // t,),
        in_specs=[
            pl.BlockSpec((nb, t, D_MODEL), lambda s: (0, s, 0)),
            pl.BlockSpec((None, nb, t, D_PLE), lambda s: (0, 0, s, 0)),
            _const_spec(vec0.shape), _const_spec(w_in_a.shape), _const_spec(wg.shape),
            _const_spec(w_out_a.shape), _const_spec(wpg0.shape), _const_spec(wpp0.shape),
        ],
        out_specs=[
            pl.BlockSpec((nb, t, D_MODEL), lambda s: (0, s, 0)),
            pl.BlockSpec((nb, SUBLANES, D_RNN), lambda s: (0, 0, 0)),
            pl.BlockSpec((nb, SUBLANES, D_RNN), lambda s: (0, 0, 0)),
        ],
        out_shape=[
            jax.ShapeDtypeStruct((nb, seq, D_MODEL), F32),
            jax.ShapeDtypeStruct((nb, SUBLANES, D_RNN), F32),
            jax.ShapeDtypeStruct((nb, SUBLANES, D_RNN), F32),
        ],
        scratch_shapes=[pltpu.VMEM((nb, 1, D_RNN), F32), pltpu.VMEM((nslab, HDR + t, LANES), F32)]
        + [pltpu.VMEM((nslab, t, LANES), F32)] * 4
        + [pltpu.VMEM((nslab, rows, LANES), F32)
           for m in (t // CHUNK, t // CHUNK ** 2) for rows in (m, m, HDR + m)],
        compiler_params=pltpu.CompilerParams(
            dimension_semantics=("arbitrary",), vmem_limit_bytes=VMEM_LIMIT),
        name="prompt_l0",
    )(x, p_all, vec0, w_in_a, wg, w_out_a, wpg0, wpp0)


def _slope(head):
    return 2.0 ** (-8.0 * (head + 1) / N_HEADS)


def _build_prompt_bias(bias_ref):
    shape = (BLOCK, 4 * BLOCK)
    qi = lax.broadcasted_iota(jnp.int32, shape, 0)
    col = lax.broadcasted_iota(jnp.int32, shape, 1)
    si = col & (2 * BLOCK - 1)
    second = col >= 2 * BLOCK
    dist = BLOCK + qi - si
    valid = (dist >= 0) & (dist < WINDOW)
    valid_first = valid & (si >= BLOCK)
    distf = dist.astype(F32)
    for pair in range(N_HEADS // 2):
        slope = jnp.where(second, _slope(2 * pair + 1), _slope(2 * pair))
        b = -(slope * distf)
        bias_ref[0, pair] = jnp.where(valid_first, b, NEG_INF)
        bias_ref[1, pair] = jnp.where(valid, b, NEG_INF)


def _both_halves(x2, kv, lo):
    if kv % 2 == 0:
        low = jnp.where(lo, x2, 0.0)
        high = pltpu.roll(low, HEAD_DIM, 1)
    else:
        high = jnp.where(lo, 0.0, x2)
        low = pltpu.roll(high, HEAD_DIM, 1)
    return jnp.concatenate([low, high], axis=0).astype(BF16)


def _pair_scores(q2, k_both, bias):
    return lax.dot_general(q2, k_both, (((1,), (1,)), ((), ())), preferred_element_type=F32) + bias


def _pair_values(s2, v_both, sinks, lo):
    es, inv = [], []
    for half in range(2):
        sh = s2[:, half * 2 * BLOCK:(half + 1) * 2 * BLOCK]
        m = jnp.maximum(jnp.max(sh, axis=-1, keepdims=True), sinks[half])
        e = jnp.exp(sh - m)
        denom = jnp.sum(e, axis=-1, keepdims=True) + jnp.exp(sinks[half] - m)
        es.append(e.astype(BF16))
        inv.append(1.0 / denom)
    o2 = jnp.dot(jnp.concatenate(es, axis=1), v_both, preferred_element_type=F32)
    return o2 * jnp.where(lo, inv[0], inv[1])


PROJ_W = Q_W + 2 * KV_W + Q_W
NCOL = 256


def _spread(n_tasks, n_slots):
    return [(s + 1) * n_tasks // n_slots - s * n_tasks // n_slots for s in range(n_slots)]


def _prompt_l1_kernel(sink_ref, h1_ref, h1n_ref, p_ref, vec_ref, win_ref, wout_ref, wpg_ref, wpp_ref,
                      y_ref, kl_ref, vl_ref, kprev, vprev, bias_ref, proj_s):
    nstreams, t_rows = h1_ref.shape[0], h1_ref.shape[1]
    assert nstreams == 2
    nblk = t_rows // BLOCK
    nchunk = PROJ_W // NCOL
    s_idx = pl.program_id(0)

    def proj_tasks(n, src):
        state = {}

        def norm():
            state["u"] = _rms(src(), vec_ref[V1_NORM:V1_NORM + 1]).astype(BF16)

        def chunk(c):
            proj_s[n, c] = jnp.dot(state["u"], win_ref[:, c * NCOL:(c + 1) * NCOL], preferred_element_type=F32)

        return [norm] + [functools.partial(chunk, c) for c in range(nchunk)]

    @pl.when(s_idx == 0)
    def _():
        _build_prompt_bias(bias_ref)
        kprev[...] = jnp.zeros_like(kprev)
        vprev[...] = jnp.zeros_like(vprev)
        for task in proj_tasks(0, lambda: h1_ref[0]):
            task()

    lo = lax.broadcasted_iota(jnp.int32, (1, LANES), 1) < HEAD_DIM
    first_tbl = jnp.where(s_idx == 0, 0, 1)
    streams = range(nstreams)

    def proj(n, rows, col0, width):
        c, off = divmod(col0, NCOL)
        assert off + width <= NCOL
        return proj_s[n, c, rows, off:off + width]

    def kv_block(n, j, base, carry_ref):
        if j < 0:
            return carry_ref[n]
        return proj(n, slice(j * BLOCK, (j + 1) * BLOCK), base, KV_W)

    cache = {}

    def masked_kv(n, j, kv, base, carry_ref):
        key = (n, j, kv, base)
        if key not in cache:
            lanes = slice((kv // 2) * LANES, (kv // 2 + 1) * LANES)
            prev, cur = kv_block(n, j - 1, base, carry_ref), kv_block(n, j, base, carry_ref)
            cache[key] = _both_halves(jnp.concatenate([prev[:, lanes], cur[:, lanes]], axis=0), kv, lo)
        return cache[key]

    def scores_of(n, j, pair):
        q2 = proj(n, slice(j * BLOCK, (j + 1) * BLOCK), pair * LANES, LANES)
        q2 = (q2 * QK_SCALE).astype(BF16)
        return _pair_scores(q2, masked_kv(n, j, pair * 2 // GROUP, Q_W, kprev),
                            bias_ref[first_tbl if j == 0 else 1, pair])

    def values_of(n, j, pair, s2):
        sinks = [sink_ref[0, 2 * pair + half] for half in range(2)]
        return _pair_values(s2, masked_kv(n, j, pair * 2 // GROUP, Q_W + KV_W, vprev), sinks, lo)

    pp = {}
    o_blocks = {n: [] for n in streams}

    def ple_proj_task(n):
        def run():
            pp[n] = _dot(p_ref[n], wpp_ref[...])
        return run

    def tail_tasks(n):
        state = {}
        ncol = D_MODEL // NCOL

        def start():
            k_last = kv_block(n, nblk - 1, Q_W, kprev)
            v_last = kv_block(n, nblk - 1, Q_W + KV_W, vprev)
            kprev[n] = k_last
            vprev[n] = v_last
            kl_ref[n] = k_last
            vl_ref[n] = v_last
            gate = jnp.concatenate([proj_s[n, c] for c in range((Q_W + 2 * KV_W) // NCOL, nchunk)], axis=1)
            state["hg"] = (jnp.concatenate(o_blocks[n], axis=0) * _silu(gate)).astype(BF16)

        def out_chunk(c):
            cols = slice(c * NCOL, (c + 1) * NCOL)
            state["h2", c] = h1_ref[n, :, cols] + jnp.dot(state["hg"], wout_ref[:, cols], preferred_element_type=F32)

        def norm():
            state["h2"] = jnp.concatenate([state["h2", c] for c in range(ncol)], axis=1)
            state["nrm"] = _rms(state["h2"], vec_ref[V1_PLE:V1_PLE + 1]).astype(BF16)

        def gate_chunk(c):
            cols = slice(c * NCOL, (c + 1) * NCOL)
            gl = jnp.dot(state["nrm"], wpg_ref[:, cols], preferred_element_type=F32)
            state["out", c] = state["h2"][:, cols] + _sigmoid(gl) * pp[n][:, cols]

        def final():
            out = jnp.concatenate([state["out", c] for c in range(ncol)], axis=1)
            y_ref[n] = _rms(out, vec_ref[V1_FINAL:V1_FINAL + 1])

        return ([start] + [functools.partial(out_chunk, c) for c in range(ncol)] + [norm]
                + [functools.partial(gate_chunk, c) for c in range(ncol)] + [final])

    per_block = N_HEADS // 2
    units = [(n, j, pair) for n in streams for j in range(nblk) for pair in range(per_block)]
    per_stream = len(units) // nstreams
    fill_first = proj_tasks(1, lambda: h1_ref[1]) + [ple_proj_task(0), ple_proj_task(1)]
    fill_second = tail_tasks(0)
    plan = ([fill_first.pop(0) for _ in range(k)] for k in _spread(len(fill_first), per_stream))
    plan = list(plan) + [[] for _ in range(ATTN_LAG)]
    plan += [[fill_second.pop(0) for _ in range(k)] for k in _spread(len(fill_second), per_stream)]
    scores, cols = {}, {}
    for t in range(len(units) + ATTN_LAG):
        if t < len(units):
            scores[t] = scores_of(*units[t])
        if t >= ATTN_LAG:
            n, j, pair = units[t - ATTN_LAG]
            cols[pair] = values_of(n, j, pair, scores.pop(t - ATTN_LAG))
            if len(cols) == per_block:
                o_blocks[n].append(jnp.concatenate([cols[c] for c in range(per_block)], axis=1))
                cols = {}
        for task in plan[t]:
            task()

    ahead = proj_tasks(0, lambda: h1n_ref[...])
    for pair in zip(ahead, tail_tasks(1)):
        for task in pair:
            task()


def _prompt_l1(sinks_p, h1, p_all, vec1, w_in_b, w_out_b, wpg1, wpp1):
    nb, seq, _ = h1.shape
    t = PROMPT_TILE
    return pl.pallas_call(
        _prompt_l1_kernel,
        grid=(seq // t,),
        in_specs=[
            pl.BlockSpec(memory_space=pltpu.SMEM),
            pl.BlockSpec((nb, t, D_MODEL), lambda s: (0, s, 0)),
            pl.BlockSpec((None, t, D_MODEL), lambda s: (0, jnp.minimum(s + 1, seq // t - 1), 0)),
            pl.BlockSpec((None, nb, t, D_PLE), lambda s: (1, 0, s, 0)),
            _const_spec(vec1.shape), _const_spec(w_in_b.shape), _const_spec(w_out_b.shape),
            _const_spec(wpg1.shape), _const_spec(wpp1.shape),
        ],
        out_specs=[
            pl.BlockSpec((nb, t, D_MODEL), lambda s: (0, s, 0)),
            pl.BlockSpec((nb, BLOCK, KV_W), lambda s: (0, 0, 0)),
            pl.BlockSpec((nb, BLOCK, KV_W), lambda s: (0, 0, 0)),
        ],
        out_shape=[
            jax.ShapeDtypeStruct((nb, seq, D_MODEL), F32),
            jax.ShapeDtypeStruct((nb, BLOCK, KV_W), F32),
            jax.ShapeDtypeStruct((nb, BLOCK, KV_W), F32),
        ],
        scratch_shapes=[
            pltpu.VMEM((nb, BLOCK, KV_W), F32), pltpu.VMEM((nb, BLOCK, KV_W), F32),
            pltpu.VMEM((2, 2 * GROUP, BLOCK, 4 * BLOCK), F32),
            pltpu.VMEM((nb, PROJ_W // NCOL, t, NCOL), F32),
        ],
        compiler_params=pltpu.CompilerParams(
            dimension_semantics=("arbitrary",), vmem_limit_bytes=VMEM_LIMIT),
        name="prompt_l1",
    )(sinks_p, h1, h1, p_all, vec1, w_in_b, w_out_b, wpg1, wpp1)


def _sample_l0_kernel(x_ref, p_ref, h0_ref, cst_ref, vec_ref, vec1_ref, win_ref, wg_ref, wout_ref, wpg_ref, wpp_ref,
                      winb_ref, h1_ref, hl_ref, cnew_ref, q_ref, k_ref, v_ref, gate_ref):
    nseq = h0_ref.shape[0]
    steps = x_ref.shape[0] // nseq
    x = x_ref[...]
    xg = _dot(_rms(x, vec_ref[V0_NORM:V0_NORM + 1]), win_ref[...])
    xb = xg[:, :D_RNN]
    gate = xg[:, D_RNN:]
    hist = [cst_ref[k] for k in range(CONV_W - 1)] + [xb[t * nseq:(t + 1) * nseq] for t in range(steps)]
    xc = []
    for t in range(steps):
        acc = vec_ref[V0_CB:V0_CB + 1] + vec_ref[V0_CW:V0_CW + 1] * hist[t]
        for k in range(1, CONV_W):
            acc = acc + vec_ref[V0_CW + k:V0_CW + k + 1] * hist[t + k]
        xc.append(acc)
    for k in range(CONV_W - 1):
        cnew_ref[k] = hist[steps + k]
    xc = jnp.concatenate(xc, axis=0)
    a, b = _lru_coeffs(xc, _lru_gate_logits(xc, wg_ref), vec_ref[V0_BR:V0_BR + 1], vec_ref[V0_BI:V0_BI + 1],
                       vec_ref[V0_LAM:V0_LAM + 1])
    h = h0_ref[...]
    hs = []
    for t in range(steps):
        h = a[t * nseq:(t + 1) * nseq] * h + b[t * nseq:(t + 1) * nseq]
        hs.append(h)
    hl_ref[...] = h
    h1 = x + _dot(jnp.concatenate(hs, axis=0) * _silu(gate), wout_ref[...])
    h1 = _ple(h1, p_ref[...], vec_ref[V0_PLE:V0_PLE + 1], wpg_ref, wpp_ref)
    h1_ref[...] = h1
    proj = _dot(_rms(h1, vec1_ref[V1_NORM:V1_NORM + 1]), winb_ref[...])
    q_ref[...] = proj[:, :Q_W] * QK_SCALE
    k_ref[...] = proj[:, Q_W:Q_W + KV_W]
    v_ref[...] = proj[:, Q_W + KV_W:Q_W + 2 * KV_W]
    gate_ref[...] = proj[:, Q_W + 2 * KV_W:]


def _sample_l0(x_tm, p0_tm, h0, cst_tm, vec0, vec1, w_in_a, wg, w_out_a, wpg0, wpp0, w_in_b):
    rows = x_tm.shape[0]
    nseq = h0.shape[0]
    args = (x_tm, p0_tm, h0, cst_tm, vec0, vec1, w_in_a, wg, w_out_a, wpg0, wpp0, w_in_b)
    out_shapes = [(rows, D_MODEL), (nseq, D_RNN), (CONV_W - 1, nseq, D_RNN),
                  (rows, Q_W), (rows, KV_W), (rows, KV_W), (rows, Q_W)]
    return pl.pallas_call(
        _sample_l0_kernel,
        grid=(1,),
        in_specs=[_const_spec(a.shape) for a in args],
        out_specs=[pl.BlockSpec(s, lambda *_, n=len(s): (0,) * n) for s in out_shapes],
        out_shape=[jax.ShapeDtypeStruct(s, F32) for s in out_shapes],
        compiler_params=pltpu.CompilerParams(dimension_semantics=("arbitrary",), vmem_limit_bytes=VMEM_LIMIT),
        name="sample_l0",
    )(*args)


def _sample_attn_kernel(q_ref, knt_ref, vnt_ref, kc_ref, vc_ref, rowinfo_ref, o_ref, nk_ref, nv_ref):
    nseq = q_ref.shape[0]
    nq = q_ref.shape[1]
    steps = nq // GROUP
    rows = N_KV * nq
    sink = rowinfo_ref[:, 0:1]
    slope = rowinfo_ref[:, 1:2]
    tq = rowinfo_ref[:, 2:3]
    col = lax.broadcasted_iota(jnp.int32, (rows, 2 * WINDOW), 1).astype(F32)
    dist = WINDOW + tq - col
    valid = (dist >= 0.0) & (dist < WINDOW)
    bias = jnp.where(valid, -(slope * dist), NEG_INF)
    lane_grp = lax.broadcasted_iota(jnp.int32, (1, KV_W), 1) // HEAD_DIM
    new_lane = lax.broadcasted_iota(jnp.int32, (1, WINDOW), 1) < steps
    knt_all = knt_ref[...]
    vnt_all = vnt_ref[...]

    for bl in range(nseq):
        q = q_ref[bl]
        q_full = jnp.concatenate([jnp.where(lane_grp == g, q, 0.0) for g in range(N_KV)], axis=0).astype(BF16)
        shift = (LANES - steps * bl) % LANES
        knt = knt_all if shift == 0 else pltpu.roll(knt_all, shift, 1)
        vnt = vnt_all if shift == 0 else pltpu.roll(vnt_all, shift, 1)
        kt_ext = jnp.concatenate([kc_ref[bl], jnp.where(new_lane, knt, 0.0)], axis=1)
        vt_ext = jnp.concatenate([vc_ref[bl], jnp.where(new_lane, vnt, 0.0)], axis=1)
        s = jnp.dot(q_full, kt_ext.astype(BF16), preferred_element_type=F32) + bias
        m = jnp.maximum(jnp.max(s, axis=-1, keepdims=True), sink)
        e = jnp.exp(s - m)
        denom = jnp.sum(e, axis=-1, keepdims=True) + jnp.exp(sink - m)
        o = lax.dot_general(e.astype(BF16), vt_ext.astype(BF16), (((1,), (1,)), ((), ())),
                            preferred_element_type=F32) * (1.0 / denom)
        acc = jnp.where(lane_grp == 0, o[:nq], 0.0)
        for g in range(1, N_KV):
            acc = acc + jnp.where(lane_grp == g, o[g * nq:(g + 1) * nq], 0.0)
        o_ref[bl] = acc
        nk_ref[bl] = pltpu.roll(kt_ext, 2 * WINDOW - steps, 1)[:, :WINDOW]
        nv_ref[bl] = pltpu.roll(vt_ext, 2 * WINDOW - steps, 1)[:, :WINDOW]


def _sample_attn(q16, knt, vnt, kc, vc, rowinfo):
    nseq_total, nq, _ = q16.shape
    sb = SAMPLE_SEQS
    blk3 = lambda i: (i, 0, 0)
    return pl.pallas_call(
        _sample_attn_kernel,
        grid=(nseq_total // sb,),
        in_specs=[
            pl.BlockSpec((sb, nq, KV_W), blk3),
            pl.BlockSpec((None, KV_W, LANES), blk3),
            pl.BlockSpec((None, KV_W, LANES), blk3),
            pl.BlockSpec((sb, KV_W, WINDOW), blk3),
            pl.BlockSpec((sb, KV_W, WINDOW), blk3),
            pl.BlockSpec(rowinfo.shape, lambda i: (0, 0)),
        ],
        out_specs=[
            pl.BlockSpec((sb, nq, KV_W), blk3),
            pl.BlockSpec((sb, KV_W, WINDOW), blk3),
            pl.BlockSpec((sb, KV_W, WINDOW), blk3),
        ],
        out_shape=[
            jax.ShapeDtypeStruct((nseq_total, nq, KV_W), F32),
            jax.ShapeDtypeStruct((nseq_total, KV_W, WINDOW), F32),
            jax.ShapeDtypeStruct((nseq_total, KV_W, WINDOW), F32),
        ],
        compiler_params=pltpu.CompilerParams(dimension_semantics=("arbitrary",), vmem_limit_bytes=VMEM_LIMIT),
        name="sample_attn",
    )(q16, knt, vnt, kc, vc, rowinfo)


def _sample_tail_kernel(o_ref, gate_ref, h1_ref, p_ref, vec_ref, wout_ref, wpg_ref, wpp_ref, y_ref):
    h1 = h1_ref[...]
    h2 = h1 + _dot(o_ref[...] * _silu(gate_ref[...]), wout_ref[...])
    h2 = _ple(h2, p_ref[...], vec_ref[V1_PLE:V1_PLE + 1], wpg_ref, wpp_ref)
    y_ref[...] = _rms(h2, vec_ref[V1_FINAL:V1_FINAL + 1])


def _sample_tail(o_tm, gate, h1, p1_tm, vec1, w_out_b, wpg1, wpp1):
    args = (o_tm, gate, h1, p1_tm, vec1, w_out_b, wpg1, wpp1)
    return pl.pallas_call(
        _sample_tail_kernel,
        grid=(1,),
        in_specs=[_const_spec(a.shape) for a in args],
        out_specs=pl.BlockSpec(h1.shape, lambda i: (0, 0)),
        out_shape=jax.ShapeDtypeStruct(h1.shape, F32),
        compiler_params=pltpu.CompilerParams(dimension_semantics=("arbitrary",), vmem_limit_bytes=VMEM_LIMIT),
        name="sample_tail",
    )(*args)


def _pad_rows(v, rows):
    return jnp.concatenate([v, jnp.zeros((rows - v.shape[0], v.shape[1]), v.dtype)], axis=0)


def kernel(x_prompt, x_sample, p_prompt, p_sample, state_lru_h, state_conv, cache_k, cache_v, norm_g, final_norm_g, ple_norm_g, w_ple_gate, w_ple_proj, w_in_a, conv_w_a, conv_b_a, w_rgate, b_rgate, w_igate, b_igate, lru_lambda, w_out_a, w_in_b, sinks, w_out_b):
    assert norm_g.shape[0] == 2 and w_in_a.shape[0] == 1 and w_in_b.shape[0] == 1
    nseq, steps, _ = x_sample.shape

    vec0 = _pad_rows(jnp.concatenate([norm_g[0:1], ple_norm_g[0:1], conv_b_a, b_rgate, b_igate, lru_lambda,
                                      conv_w_a[0]], axis=0), 2 * SUBLANES)
    vec1 = _pad_rows(jnp.concatenate([norm_g[1:2], ple_norm_g[1:2], final_norm_g[None]], axis=0), SUBLANES)
    w_in_a_b = w_in_a[0].astype(BF16)
    w_out_a_b = w_out_a[0].astype(BF16)
    wg = jnp.concatenate([w_rgate[0], w_igate[0]], axis=-1).astype(BF16)
    wpg = w_ple_gate.astype(BF16)
    wpp = w_ple_proj.astype(BF16)
    w_in_b_b = w_in_b[0].astype(BF16)
    w_out_b_b = w_out_b[0].astype(BF16)
    sinks_p = sinks[0].reshape(1, N_HEADS)

    h1_p, hl_p, tail_p = _prompt_l0(x_prompt, p_prompt, vec0, w_in_a_b, wg, w_out_a_b, wpg[0], wpp[0])
    y_prompt, kl_p, vl_p = _prompt_l1(sinks_p, h1_p, p_prompt, vec1, w_in_b_b, w_out_b_b, wpg[1], wpp[1])
    nbp = x_prompt.shape[0]
    new_h_p = hl_p[:, SUBLANES - 1][None]
    new_conv_p = tail_p[:, SUBLANES - (CONV_W - 1):][None]
    new_k_p = kl_p.reshape(1, nbp, WINDOW, N_KV, HEAD_DIM)
    new_v_p = vl_p.reshape(1, nbp, WINDOW, N_KV, HEAD_DIM)

    rows = steps * nseq
    x_tm = jnp.swapaxes(x_sample, 0, 1).reshape(rows, D_MODEL)
    p_tm = jnp.swapaxes(p_sample, 1, 2).reshape(2, rows, D_PLE)
    cst_tm = jnp.swapaxes(state_conv[0], 0, 1)
    h1_s, hl_s, cnew_tm, q_s, k_s, v_s, gate_s = _sample_l0(
        x_tm, p_tm[0], state_lru_h[0], cst_tm, vec0, vec1, w_in_a_b, wg, w_out_a_b, wpg[0], wpp[0], w_in_b_b)

    sb = SAMPLE_SEQS
    q16 = q_s.reshape(steps, nseq, N_KV, GROUP, HEAD_DIM).transpose(1, 3, 0, 2, 4).reshape(nseq, GROUP * steps, KV_W)

    def new_rows_t(a):
        a = a.reshape(steps, nseq // sb, sb, KV_W).transpose(1, 3, 2, 0).reshape(nseq // sb, KV_W, sb * steps)
        return jnp.concatenate([a, jnp.zeros((nseq // sb, KV_W, LANES - sb * steps), a.dtype)], axis=-1)

    kc = cache_k[0].transpose(0, 2, 3, 1).reshape(nseq, KV_W, WINDOW)
    vc = cache_v[0].transpose(0, 2, 3, 1).reshape(nseq, KV_W, WINDOW)
    head = jnp.arange(N_KV * GROUP * steps) // steps
    slopes = jnp.exp2(-8.0 * (head + 1).astype(F32) / N_HEADS)
    tq = (jnp.arange(N_KV * GROUP * steps) % steps).astype(F32)
    rowinfo = jnp.stack([sinks[0][head], slopes, tq], axis=1)
    rowinfo = jnp.concatenate([rowinfo, jnp.zeros((rowinfo.shape[0], LANES - 3), F32)], axis=1)
    o16, nk, nv = _sample_attn(q16, new_rows_t(k_s), new_rows_t(v_s), kc, vc, rowinfo)

    o_tm = o16.reshape(nseq, GROUP, steps, N_KV, HEAD_DIM).transpose(2, 0, 3, 1, 4).reshape(rows, Q_W)
    y_tm = _sample_tail(o_tm, gate_s, h1_s, p_tm[1], vec1, w_out_b_b, wpg[1], wpp[1])
    y_sample = jnp.swapaxes(y_tm.reshape(steps, nseq, D_MODEL), 0, 1)
    new_conv_s = jnp.swapaxes(cnew_tm, 0, 1)[None]
    new_k_s = nk.reshape(nseq, N_KV, HEAD_DIM, WINDOW).transpose(0, 3, 1, 2)[None]
    new_v_s = nv.reshape(nseq, N_KV, HEAD_DIM, WINDOW).transpose(0, 3, 1, 2)[None]

    return (y_prompt, y_sample, new_h_p, new_conv_p, new_k_p, new_v_p,
            hl_s[None], new_conv_s, new_k_s, new_v_s)
```

```python
import functools

import jax
import jax.numpy as jnp
from jax import lax
from jax.experimental import pallas as pl
from jax.experimental.pallas import tpu as pltpu

D_MODEL = 1024
D_RNN = 1024
D_PLE = 256
N_LRU_BLOCKS = 8
LRU_BLOCK = 128
CONV_W = 4
LRU_C = 8.0
EPS = 1e-6
HEAD_DIM = 64
N_HEADS = 16
N_KV = 4
GROUP = 4
Q_W = 1024
KV_W = 256
WINDOW = 128
BLOCK = 128
NEG_INF = -1e30
QK_SCALE = HEAD_DIM ** -0.5
SQRT_FLOOR = 1e-37

LANES = 128
SUBLANES = 8
VMEM_LIMIT = 56 * 1024 * 1024

PROMPT_TILE = 256
SAMPLE_SEQS = 16
ATTN_LAG = 4
SAMPLE_LAG = 3

F32 = jnp.float32
BF16 = jnp.bfloat16


def _rms(x, g):
    ms = jnp.mean(x * x, axis=-1, keepdims=True)
    return x * lax.rsqrt(ms + EPS) * g


def _sigmoid(x):
    return 0.5 * jnp.tanh(0.5 * x) + 0.5


def _silu(x):
    return x * _sigmoid(x)


def _dot(a, w):
    return jnp.dot(a.astype(BF16), w, preferred_element_type=F32)


def _softplus(z):
    return jnp.maximum(z, 0.0) + jnp.log1p(jnp.exp(-jnp.abs(z)))


def _lru_gate_logits(xc, wg_ref):
    xcb = xc.astype(BF16)
    zr, zi = [], []
    for n in range(N_LRU_BLOCKS):
        z = jnp.dot(xcb[:, n * LRU_BLOCK:(n + 1) * LRU_BLOCK], wg_ref[n], preferred_element_type=F32)
        zr.append(z[:, :LRU_BLOCK])
        zi.append(z[:, LRU_BLOCK:])
    return jnp.concatenate(zr, axis=1), jnp.concatenate(zi, axis=1)


def _lru_coeffs(xc, logits, b_r, b_i, lam):
    r = _sigmoid(logits[0] + b_r)
    i = _sigmoid(logits[1] + b_i)
    log_a = (-LRU_C * _softplus(-lam)) * r
    a = jnp.exp(log_a)
    y = -jnp.tanh(log_a) * (a * a + 1.0)
    b = (y * lax.rsqrt(jnp.maximum(y, SQRT_FLOOR))) * (i * xc)
    return a, b


def _ple(h, p, g, wpg_ref, wpp_ref):
    gate = _sigmoid(_dot(_rms(h, g), wpg_ref[...]))
    return h + gate * _dot(p, wpp_ref[...])


CHUNK = 4
HDR = SUBLANES


def _local_scans(aq, bq):
    big_a, big_b = [aq[0]], [bq[0]]
    for q in range(1, CHUNK):
        big_a.append(aq[q] * big_a[q - 1])
        big_b.append(aq[q] * big_b[q - 1] + bq[q])
    return big_a, big_b


def _carry_in(i, a_tot, b_tot, h0, levels):
    m = a_tot.shape[0]
    if m == CHUNK:
        h, cin = h0, []
        for r in range(CHUNK):
            cin.append(h)
            h = a_tot[r:r + 1] * h + b_tot[r:r + 1]
        return jnp.concatenate(cin, axis=0), h
    a_buf, b_buf, h_buf = levels[0]
    a_buf[i] = a_tot
    b_buf[i] = b_tot
    mq = m // CHUNK
    aq = [a_buf[i, pl.ds(q, mq, stride=CHUNK), :] for q in range(CHUNK)]
    bq = [b_buf[i, pl.ds(q, mq, stride=CHUNK), :] for q in range(CHUNK)]
    big_a, big_b = _local_scans(aq, bq)
    cin, h_last = _carry_in(i, big_a[-1], big_b[-1], h0, levels[1:])
    h_buf[i, HDR - 1:HDR, :] = h0
    for q in range(CHUNK):
        h_buf[i, pl.ds(HDR + q, mq, stride=CHUNK), :] = big_b[q] + big_a[q] * cin
    return h_buf[i, HDR - 1:HDR - 1 + m, :], h_last


V0_NORM, V0_PLE, V0_CB, V0_BR, V0_BI, V0_LAM, V0_CW = 0, 1, 2, 3, 4, 5, 6
V1_NORM, V1_PLE, V1_FINAL = 0, 1, 2


def _prompt_l0_kernel(x_ref, p_ref, vec_ref, win_ref, wg_ref, wout_ref, wpg_ref, wpp_ref,
                      h1_ref, hl_ref, tail_ref,
                      hcar, xb_s, xc_s, zr_s, zi_s, h_s, l2a, l2b, l2h, l3a, l3b, l3h):
    t_rows = x_ref.shape[1]
    nslab = D_RNN // LANES
    levels = ((l2a, l2b, l2h), (l3a, l3b, l3h))

    @pl.when(pl.program_id(0) == 0)
    def _():
        hcar[...] = jnp.zeros_like(hcar)
        xb_s[:, 0:HDR, :] = jnp.zeros((xb_s.shape[0], HDR, LANES), F32)

    st = [dict() for _ in range(x_ref.shape[0])]

    def in_proj(n):
        st[n]["xg"] = _dot(_rms(x_ref[n], vec_ref[V0_NORM:V0_NORM + 1]), win_ref[...])
        st[n]["pp"] = _dot(p_ref[n], wpp_ref[...])

    def conv_gates(n):
        xg = st[n]["xg"]
        tails = []
        for l in range(nslab):
            i = n * nslab + l
            lanes = slice(l * LANES, (l + 1) * LANES)
            xb = xg[:, lanes]
            xb_s[i, HDR:HDR + t_rows, :] = xb
            acc = vec_ref[V0_CB:V0_CB + 1, lanes] + vec_ref[V0_CW + 3:V0_CW + 4, lanes] * xb
            for k in range(1, CONV_W):
                acc = acc + vec_ref[V0_CW + 3 - k:V0_CW + 4 - k, lanes] * xb_s[i, HDR - k:HDR - k + t_rows, :]
            xc_s[i] = acc
            z = jnp.dot(acc.astype(BF16), wg_ref[l], preferred_element_type=F32)
            zr_s[i] = z[:, :LRU_BLOCK]
            zi_s[i] = z[:, LRU_BLOCK:]
            tail = xb[t_rows - HDR:]
            xb_s[i, 0:HDR, :] = tail
            tails.append(tail)
        tail_ref[n] = jnp.concatenate(tails, axis=1)

    def recurrence(n):
        nchunk = t_rows // CHUNK
        for l in range(nslab):
            i = n * nslab + l
            lanes = slice(l * LANES, (l + 1) * LANES)
            aq, bq = [], []
            for q in range(CHUNK):
                rows = pl.ds(q, nchunk, stride=CHUNK)
                a, b = _lru_coeffs(xc_s[i, rows, :], (zr_s[i, rows, :], zi_s[i, rows, :]),
                                   vec_ref[V0_BR:V0_BR + 1, lanes], vec_ref[V0_BI:V0_BI + 1, lanes],
                                   vec_ref[V0_LAM:V0_LAM + 1, lanes])
                aq.append(a)
                bq.append(b)
            big_a, big_b = _local_scans(aq, bq)
            cin, h_last = _carry_in(i, big_a[-1], big_b[-1], hcar[n, :, lanes], levels)
            for q in range(CHUNK):
                h_s[i, pl.ds(q, nchunk, stride=CHUNK), :] = big_b[q] + big_a[q] * cin
            hcar[n, :, lanes] = h_last
        h = jnp.concatenate([h_s[n * nslab + l] for l in range(nslab)], axis=1)
        hl_ref[n] = h[t_rows - SUBLANES:]
        st[n]["hg"] = h * _silu(st[n]["xg"][:, D_RNN:])

    def out_proj(n):
        st[n]["h1"] = x_ref[n] + _dot(st[n]["hg"], wout_ref[...])

    def ple(n):
        h1 = st[n]["h1"]
        gl = _dot(_rms(h1, vec_ref[V0_PLE:V0_PLE + 1]), wpg_ref[...])
        h1_ref[n] = h1 + _sigmoid(gl) * st[n]["pp"]

    for stage in (in_proj, conv_gates):
        for n in range(len(st)):
            stage(n)
    for n in range(len(st)):
        recurrence(n)
        out_proj(n)
    for n in range(len(st)):
        ple(n)


def _const_spec(shape):
    return pl.BlockSpec(shape, lambda *_: (0,) * len(shape), pipeline_mode=pl.Buffered(1))


def _prompt_l0(x, p_all, vec0, w_in_a, wg, w_out_a, wpg0, wpp0):
    nb, seq, _ = x.shape
    t = PROMPT_TILE
    assert t % CHUNK ** 3 == 0 and t // CHUNK ** 3 == CHUNK, "three strided levels, then a CHUNK-row serial scan"
    nslab = nb * (D_RNN // LANES)
    return pl.pallas_call(
        _prompt_l0_kernel,
        grid=(seq // t,),
        in_specs=[
            pl.BlockSpec((nb, t, D_MODEL), lambda s: (0, s, 0)),
            pl.BlockSpec((None, nb, t, D_PLE), lambda s: (0, 0, s, 0)),
            _const_spec(vec0.shape), _const_spec(w_in_a.shape), _const_spec(wg.shape),
            _const_spec(w_out_a.shape), _const_spec(wpg0.shape), _const_spec(wpp0.shape),
        ],
        out_specs=[
            pl.BlockSpec((nb, t, D_MODEL), lambda s: (0, s, 0)),
            pl.BlockSpec((nb, SUBLANES, D_RNN), lambda s: (0, 0, 0)),
            pl.BlockSpec((nb, SUBLANES, D_RNN), lambda s: (0, 0, 0)),
        ],
        out_shape=[
            jax.ShapeDtypeStruct((nb, seq, D_MODEL), F32),
            jax.ShapeDtypeStruct((nb, SUBLANES, D_RNN), F32),
            jax.ShapeDtypeStruct((nb, SUBLANES, D_RNN), F32),
        ],
        scratch_shapes=[pltpu.VMEM((nb, 1, D_RNN), F32), pltpu.VMEM((nslab, HDR + t, LANES), F32)]
        + [pltpu.VMEM((nslab, t, LANES), F32)] * 4
        + [pltpu.VMEM((nslab, rows, LANES), F32)
           for m in (t // CHUNK, t // CHUNK ** 2) for rows in (m, m, HDR + m)],
        compiler_params=pltpu.CompilerParams(
            dimension_semantics=("arbitrary",), vmem_limit_bytes=VMEM_LIMIT),
        name="prompt_l0",
    )(x, p_all, vec0, w_in_a, wg, w_out_a, wpg0, wpp0)


def _slope(head):
    return 2.0 ** (-8.0 * (head + 1) / N_HEADS)


def _build_prompt_bias(bias_ref):
    shape = (BLOCK, 4 * BLOCK)
    qi = lax.broadcasted_iota(jnp.int32, shape, 0)
    col = lax.broadcasted_iota(jnp.int32, shape, 1)
    si = col & (2 * BLOCK - 1)
    second = col >= 2 * BLOCK
    dist = BLOCK + qi - si
    valid = (dist >= 0) & (dist < WINDOW)
    valid_first = valid & (si >= BLOCK)
    distf = dist.astype(F32)
    for pair in range(N_HEADS // 2):
        slope = jnp.where(second, _slope(2 * pair + 1), _slope(2 * pair))
        b = -(slope * distf)
        bias_ref[0, pair] = jnp.where(valid_first, b, NEG_INF)
        bias_ref[1, pair] = jnp.where(valid, b, NEG_INF)


def _both_halves(x2, kv, lo):
    if kv % 2 == 0:
        low = jnp.where(lo, x2, 0.0)
        high = pltpu.roll(low, HEAD_DIM, 1)
    else:
        high = jnp.where(lo, 0.0, x2)
        low = pltpu.roll(high, HEAD_DIM, 1)
    return jnp.concatenate([low, high], axis=0).astype(BF16)


def _pair_scores(q2, k_both, bias):
    return lax.dot_general(q2, k_both, (((1,), (1,)), ((), ())), preferred_element_type=F32) + bias


def _pair_values(s2, v_both, sinks, lo):
    es, inv = [], []
    for half in range(2):
        sh = s2[:, half * 2 * BLOCK:(half + 1) * 2 * BLOCK]
        m = jnp.maximum(jnp.max(sh, axis=-1, keepdims=True), sinks[half])
        e = jnp.exp(sh - m)
        denom = jnp.sum(e, axis=-1, keepdims=True) + jnp.exp(sinks[half] - m)
        es.append(e.astype(BF16))
        inv.append(1.0 / denom)
    o2 = jnp.dot(jnp.concatenate(es, axis=1), v_both, preferred_element_type=F32)
    return o2 * jnp.where(lo, inv[0], inv[1])


PROJ_W = Q_W + 2 * KV_W + Q_W
NCOL = 256


def _spread(n_tasks, n_slots):
    return [(s + 1) * n_tasks // n_slots - s * n_tasks // n_slots for s in range(n_slots)]


def _prompt_l1_kernel(sink_ref, h1_ref, h1n_ref, p_ref, vec_ref, win_ref, wout_ref, wpg_ref, wpp_ref,
                      y_ref, kl_ref, vl_ref, kprev, vprev, bias_ref, proj_s):
    nstreams, t_rows = h1_ref.shape[0], h1_ref.shape[1]
    assert nstreams == 2
    nblk = t_rows // BLOCK
    nchunk = PROJ_W // NCOL
    s_idx = pl.program_id(0)

    def proj_tasks(n, src):
        state = {}

        def norm():
            state["u"] = _rms(src(), vec_ref[V1_NORM:V1_NORM + 1]).astype(BF16)

        def chunk(c):
            proj_s[n, c] = jnp.dot(state["u"], win_ref[:, c * NCOL:(c + 1) * NCOL], preferred_element_type=F32)

        return [norm] + [functools.partial(chunk, c) for c in range(nchunk)]

    @pl.when(s_idx == 0)
    def _():
        _build_prompt_bias(bias_ref)
        kprev[...] = jnp.zeros_like(kprev)
        vprev[...] = jnp.zeros_like(vprev)
        for task in proj_tasks(0, lambda: h1_ref[0]):
            task()

    lo = lax.broadcasted_iota(jnp.int32, (1, LANES), 1) < HEAD_DIM
    first_tbl = jnp.where(s_idx == 0, 0, 1)
    streams = range(nstreams)

    def proj(n, rows, col0, width):
        c, off = divmod(col0, NCOL)
        assert off + width <= NCOL
        return proj_s[n, c, rows, off:off + width]

    def kv_block(n, j, base, carry_ref):
        if j < 0:
            return carry_ref[n]
        return proj(n, slice(j * BLOCK, (j + 1) * BLOCK), base, KV_W)

    cache = {}

    def masked_kv(n, j, kv, base, carry_ref):
        key = (n, j, kv, base)
        if key not in cache:
            lanes = slice((kv // 2) * LANES, (kv // 2 + 1) * LANES)
            prev, cur = kv_block(n, j - 1, base, carry_ref), kv_block(n, j, base, carry_ref)
            cache[key] = _both_halves(jnp.concatenate([prev[:, lanes], cur[:, lanes]], axis=0), kv, lo)
        return cache[key]

    def scores_of(n, j, pair):
        q2 = proj(n, slice(j * BLOCK, (j + 1) * BLOCK), pair * LANES, LANES)
        q2 = (q2 * QK_SCALE).astype(BF16)
        return _pair_scores(q2, masked_kv(n, j, pair * 2 // GROUP, Q_W, kprev),
                            bias_ref[first_tbl if j == 0 else 1, pair])

    def values_of(n, j, pair, s2):
        sinks = [sink_ref[0, 2 * pair + half] for half in range(2)]
        return _pair_values(s2, masked_kv(n, j, pair * 2 // GROUP, Q_W + KV_W, vprev), sinks, lo)

    pp = {}
    o_blocks = {n: [] for n in streams}

    def ple_proj_task(n):
        def run():
            pp[n] = _dot(p_ref[n], wpp_ref[...])
        return run

    def tail_tasks(n):
        state = {}
        ncol = D_MODEL // NCOL

        def start():
            k_last = kv_block(n, nblk - 1, Q_W, kprev)
            v_last = kv_block(n, nblk - 1, Q_W + KV_W, vprev)
            kprev[n] = k_last
            vprev[n] = v_last
            kl_ref[n] = k_last
            vl_ref[n] = v_last
            gate = jnp.concatenate([proj_s[n, c] for c in range((Q_W + 2 * KV_W) // NCOL, nchunk)], axis=1)
            state["hg"] = (jnp.concatenate(o_blocks[n], axis=0) * _silu(gate)).astype(BF16)

        def out_chunk(c):
            cols = slice(c * NCOL, (c + 1) * NCOL)
            state["h2", c] = h1_ref[n, :, cols] + jnp.dot(state["hg"], wout_ref[:, cols], preferred_element_type=F32)

        def norm():
            state["h2"] = jnp.concatenate([state["h2", c] for c in range(ncol)], axis=1)
            state["nrm"] = _rms(state["h2"], vec_ref[V1_PLE:V1_PLE + 1]).astype(BF16)

        def gate_chunk(c):
            cols = slice(c * NCOL, (c + 1) * NCOL)
            gl = jnp.dot(state["nrm"], wpg_ref[:, cols], preferred_element_type=F32)
            state["out", c] = state["h2"][:, cols] + _sigmoid(gl) * pp[n][:, cols]

        def final():
            out = jnp.concatenate([state["out", c] for c in range(ncol)], axis=1)
            y_ref[n] = _rms(out, vec_ref[V1_FINAL:V1_FINAL + 1])

        return ([start] + [functools.partial(out_chunk, c) for c in range(ncol)] + [norm]
                + [functools.partial(gate_chunk, c) for c in range(ncol)] + [final])

    per_block = N_HEADS // 2
    units = [(n, j, pair) for n in streams for j in range(nblk) for pair in range(per_block)]
    per_stream = len(units) // nstreams
    fill_first = proj_tasks(1, lambda: h1_ref[1]) + [ple_proj_task(0), ple_proj_task(1)]
    fill_second = tail_tasks(0)
    plan = ([fill_first.pop(0) for _ in range(k)] for k in _spread(len(fill_first), per_stream))
    plan = list(plan) + [[] for _ in range(ATTN_LAG)]
    plan += [[fill_second.pop(0) for _ in range(k)] for k in _spread(len(fill_second), per_stream)]
    scores, cols = {}, {}
    for t in range(len(units) + ATTN_LAG):
        if t < len(units):
            scores[t] = scores_of(*units[t])
        if t >= ATTN_LAG:
            n, j, pair = units[t - ATTN_LAG]
            cols[pair] = values_of(n, j, pair, scores.pop(t - ATTN_LAG))
            if len(cols) == per_block:
                o_blocks[n].append(jnp.concatenate([cols[c] for c in range(per_block)], axis=1))
                cols = {}
        for task in plan[t]:
            task()

    ahead = proj_tasks(0, lambda: h1n_ref[...])
    for pair in zip(ahead, tail_tasks(1)):
        for task in pair:
            task()


def _prompt_l1(sinks_p, h1, p_all, vec1, w_in_b, w_out_b, wpg1, wpp1):
    nb, seq, _ = h1.shape
    t = PROMPT_TILE
    return pl.pallas_call(
        _prompt_l1_kernel,
        grid=(seq // t,),
        in_specs=[
            pl.BlockSpec(memory_space=pltpu.SMEM),
            pl.BlockSpec((nb, t, D_MODEL), lambda s: (0, s, 0)),
            pl.BlockSpec((None, t, D_MODEL), lambda s: (0, jnp.minimum(s + 1, seq // t - 1), 0)),
            pl.BlockSpec((None, nb, t, D_PLE), lambda s: (1, 0, s, 0)),
            _const_spec(vec1.shape), _const_spec(w_in_b.shape), _const_spec(w_out_b.shape),
            _const_spec(wpg1.shape), _const_spec(wpp1.shape),
        ],
        out_specs=[
            pl.BlockSpec((nb, t, D_MODEL), lambda s: (0, s, 0)),
            pl.BlockSpec((nb, BLOCK, KV_W), lambda s: (0, 0, 0)),
            pl.BlockSpec((nb, BLOCK, KV_W), lambda s: (0, 0, 0)),
        ],
        out_shape=[
            jax.ShapeDtypeStruct((nb, seq, D_MODEL), F32),
            jax.ShapeDtypeStruct((nb, BLOCK, KV_W), F32),
            jax.ShapeDtypeStruct((nb, BLOCK, KV_W), F32),
        ],
        scratch_shapes=[
            pltpu.VMEM((nb, BLOCK, KV_W), F32), pltpu.VMEM((nb, BLOCK, KV_W), F32),
            pltpu.VMEM((2, 2 * GROUP, BLOCK, 4 * BLOCK), F32),
            pltpu.VMEM((nb, PROJ_W // NCOL, t, NCOL), F32),
        ],
        compiler_params=pltpu.CompilerParams(
            dimension_semantics=("arbitrary",), vmem_limit_bytes=VMEM_LIMIT),
        name="prompt_l1",
    )(sinks_p, h1, h1, p_all, vec1, w_in_b, w_out_b, wpg1, wpp1)


def _sample_l0_kernel(x_ref, p_ref, h0_ref, cst_ref, vec_ref, vec1_ref, win_ref, wg_ref, wout_ref, wpg_ref, wpp_ref,
                      winb_ref, h1_ref, hl_ref, cnew_ref, q_ref, k_ref, v_ref, gate_ref):
    nseq = h0_ref.shape[0]
    steps = x_ref.shape[0] // nseq
    x = x_ref[...]
    xg = _dot(_rms(x, vec_ref[V0_NORM:V0_NORM + 1]), win_ref[...])
    xb = xg[:, :D_RNN]
    gate = xg[:, D_RNN:]
    hist = [cst_ref[k] for k in range(CONV_W - 1)] + [xb[t * nseq:(t + 1) * nseq] for t in range(steps)]
    xc = []
    for t in range(steps):
        acc = vec_ref[V0_CB:V0_CB + 1] + vec_ref[V0_CW:V0_CW + 1] * hist[t]
        for k in range(1, CONV_W):
            acc = acc + vec_ref[V0_CW + k:V0_CW + k + 1] * hist[t + k]
        xc.append(acc)
    for k in range(CONV_W - 1):
        cnew_ref[k] = hist[steps + k]
    xc = jnp.concatenate(xc, axis=0)
    a, b = _lru_coeffs(xc, _lru_gate_logits(xc, wg_ref), vec_ref[V0_BR:V0_BR + 1], vec_ref[V0_BI:V0_BI + 1],
                       vec_ref[V0_LAM:V0_LAM + 1])
    h = h0_ref[...]
    hs = []
    for t in range(steps):
        h = a[t * nseq:(t + 1) * nseq] * h + b[t * nseq:(t + 1) * nseq]
        hs.append(h)
    hl_ref[...] = h
    h1 = x + _dot(jnp.concatenate(hs, axis=0) * _silu(gate), wout_ref[...])
    h1 = _ple(h1, p_ref[...], vec_ref[V0_PLE:V0_PLE + 1], wpg_ref, wpp_ref)
    h1_ref[...] = h1
    proj = _dot(_rms(h1, vec1_ref[V1_NORM:V1_NORM + 1]), winb_ref[...])
    q_ref[...] = proj[:, :Q_W] * QK_SCALE
    k_ref[...] = proj[:, Q_W:Q_W + KV_W]
    v_ref[...] = proj[:, Q_W + KV_W:Q_W + 2 * KV_W]
    gate_ref[...] = proj[:, Q_W + 2 * KV_W:]


def _sample_l0(x_tm, p0_tm, h0, cst_tm, vec0, vec1, w_in_a, wg, w_out_a, wpg0, wpp0, w_in_b):
    rows = x_tm.shape[0]
    nseq = h0.shape[0]
    args = (x_tm, p0_tm, h0, cst_tm, vec0, vec1, w_in_a, wg, w_out_a, wpg0, wpp0, w_in_b)
    out_shapes = [(rows, D_MODEL), (nseq, D_RNN), (CONV_W - 1, nseq, D_RNN),
                  (rows, Q_W), (rows, KV_W), (rows, KV_W), (rows, Q_W)]
    return pl.pallas_call(
        _sample_l0_kernel,
        grid=(1,),
        in_specs=[_const_spec(a.shape) for a in args],
        out_specs=[pl.BlockSpec(s, lambda *_, n=len(s): (0,) * n) for s in out_shapes],
        out_shape=[jax.ShapeDtypeStruct(s, F32) for s in out_shapes],
        compiler_params=pltpu.CompilerParams(dimension_semantics=("arbitrary",), vmem_limit_bytes=VMEM_LIMIT),
        name="sample_l0",
    )(*args)


def _sample_attn_kernel(q_ref, knt_ref, vnt_ref, kc_ref, vc_ref, rowinfo_ref, o_ref, nk_ref, nv_ref):
    nseq = q_ref.shape[0]
    nq = q_ref.shape[1]
    steps = nq // GROUP
    rows = N_KV * nq
    sink = rowinfo_ref[:, 0:1]
    slope = rowinfo_ref[:, 1:2]
    tq = rowinfo_ref[:, 2:3]
    col = lax.broadcasted_iota(jnp.int32, (rows, WINDOW), 1).astype(F32)
    dist = WINDOW + tq - col
    bias_cache = jnp.where(dist < WINDOW, -(slope * dist), NEG_INF)
    lane_grp = lax.broadcasted_iota(jnp.int32, (1, KV_W), 1) // HEAD_DIM
    keep = lax.broadcasted_iota(jnp.int32, (1, WINDOW), 1) < WINDOW - steps
    knt_all = knt_ref[...]
    vnt_all = vnt_ref[...]
    knt_bf = knt_all.astype(BF16)
    vnt_bf = vnt_all.astype(BF16)

    def scores_of(bl):
        q = q_ref[bl]
        q_full = jnp.concatenate([jnp.where(lane_grp == g, q, 0.0) for g in range(N_KV)], axis=0).astype(BF16)
        kt_ext = jnp.concatenate([kc_ref[bl].astype(BF16), knt_bf], axis=1)
        j = col - float(steps * bl)
        dist_new = tq - j
        bias_new = jnp.where((j >= 0.0) & (dist_new >= 0.0), -(slope * dist_new), NEG_INF)
        return jnp.dot(q_full, kt_ext, preferred_element_type=F32) + jnp.concatenate([bias_cache, bias_new], axis=1)

    def exp_of(s):
        m = jnp.maximum(jnp.max(s, axis=-1, keepdims=True), sink)
        return jnp.exp(s - m), jnp.exp(sink - m)

    def values_of(bl, e, e_sink):
        denom = jnp.sum(e, axis=-1, keepdims=True) + e_sink
        vt_ext = jnp.concatenate([vc_ref[bl].astype(BF16), vnt_bf], axis=1)
        o = lax.dot_general(e.astype(BF16), vt_ext, (((1,), (1,)), ((), ())),
                            preferred_element_type=F32) * (1.0 / denom)
        acc = jnp.where(lane_grp == 0, o[:nq], 0.0)
        for g in range(1, N_KV):
            acc = acc + jnp.where(lane_grp == g, o[g * nq:(g + 1) * nq], 0.0)
        o_ref[bl] = acc

    def split3(x):
        hi = x.astype(BF16)
        r1 = x - hi.astype(F32)
        mid = r1.astype(BF16)
        lo = (r1 - mid.astype(F32)).astype(BF16)
        return jnp.concatenate([hi, mid, lo], axis=1)

    knt3 = split3(knt_all)
    vnt3 = split3(vnt_all)
    src_lane = lax.broadcasted_iota(jnp.int32, (3 * LANES, WINDOW), 0) % LANES
    dst_lane = lax.broadcasted_iota(jnp.int32, (3 * LANES, WINDOW), 1)

    def shifted(cache_ref, new3, bl):
        old = pltpu.roll(cache_ref[bl], WINDOW - steps, 1)
        pick = (src_lane == dst_lane - (WINDOW - steps) + steps * bl) & (dst_lane >= WINDOW - steps)
        new = jnp.dot(new3, jnp.where(pick, 1.0, 0.0).astype(BF16), preferred_element_type=F32)
        return jnp.where(keep, old, new)

    scores = {}
    for t in range(nseq + SAMPLE_LAG):
        if t < nseq:
            scores[t] = scores_of(t)
        if t >= SAMPLE_LAG:
            bl = t - SAMPLE_LAG
            e, e_sink = exp_of(scores.pop(bl))
            nk_ref[bl] = shifted(kc_ref, knt3, bl)
            values_of(bl, e, e_sink)
            nv_ref[bl] = shifted(vc_ref, vnt3, bl)


def _sample_attn(q16, knt, vnt, kc, vc, rowinfo):
    nseq_total, nq, _ = q16.shape
    sb = SAMPLE_SEQS
    blk3 = lambda i: (i, 0, 0)
    return pl.pallas_call(
        _sample_attn_kernel,
        grid=(nseq_total // sb,),
        in_specs=[
            pl.BlockSpec((sb, nq, KV_W), blk3),
            pl.BlockSpec((None, KV_W, LANES), blk3),
            pl.BlockSpec((None, KV_W, LANES), blk3),
            pl.BlockSpec((sb, KV_W, WINDOW), blk3),
            pl.BlockSpec((sb, KV_W, WINDOW), blk3),
            pl.BlockSpec(rowinfo.shape, lambda i: (0, 0)),
        ],
        out_specs=[
            pl.BlockSpec((sb, nq, KV_W), blk3),
            pl.BlockSpec((sb, KV_W, WINDOW), blk3),
            pl.BlockSpec((sb, KV_W, WINDOW), blk3),
        ],
        out_shape=[
            jax.ShapeDtypeStruct((nseq_total, nq, KV_W), F32),
            jax.ShapeDtypeStruct((nseq_total, KV_W, WINDOW), F32),
            jax.ShapeDtypeStruct((nseq_total, KV_W, WINDOW), F32),
        ],
        compiler_params=pltpu.CompilerParams(dimension_semantics=("arbitrary",), vmem_limit_bytes=VMEM_LIMIT),
        name="sample_attn",
    )(q16, knt, vnt, kc, vc, rowinfo)


def _sample_tail_kernel(o_ref, gate_ref, h1_ref, p_ref, vec_ref, wout_ref, wpg_ref, wpp_ref, y_ref):
    h1 = h1_ref[...]
    h2 = h1 + _dot(o_ref[...] * _silu(gate_ref[...]), wout_ref[...])
    h2 = _ple(h2, p_ref[...], vec_ref[V1_PLE:V1_PLE + 1], wpg_ref, wpp_ref)
    y_ref[...] = _rms(h2, vec_ref[V1_FINAL:V1_FINAL + 1])


def _sample_tail(o_tm, gate, h1, p1_tm, vec1, w_out_b, wpg1, wpp1):
    args = (o_tm, gate, h1, p1_tm, vec1, w_out_b, wpg1, wpp1)
    return pl.pallas_call(
        _sample_tail_kernel,
        grid=(1,),
        in_specs=[_const_spec(a.shape) for a in args],
        out_specs=pl.BlockSpec(h1.shape, lambda i: (0, 0)),
        out_shape=jax.ShapeDtypeStruct(h1.shape, F32),
        compiler_params=pltpu.CompilerParams(dimension_semantics=("arbitrary",), vmem_limit_bytes=VMEM_LIMIT),
        name="sample_tail",
    )(*args)


def _pad_rows(v, rows):
    return jnp.concatenate([v, jnp.zeros((rows - v.shape[0], v.shape[1]), v.dtype)], axis=0)


def kernel(x_prompt, x_sample, p_prompt, p_sample, state_lru_h, state_conv, cache_k, cache_v, norm_g, final_norm_g, ple_norm_g, w_ple_gate, w_ple_proj, w_in_a, conv_w_a, conv_b_a, w_rgate, b_rgate, w_igate, b_igate, lru_lambda, w_out_a, w_in_b, sinks, w_out_b):
    assert norm_g.shape[0] == 2 and w_in_a.shape[0] == 1 and w_in_b.shape[0] == 1
    nseq, steps, _ = x_sample.shape

    vec0 = _pad_rows(jnp.concatenate([norm_g[0:1], ple_norm_g[0:1], conv_b_a, b_rgate, b_igate, lru_lambda,
                                      conv_w_a[0]], axis=0), 2 * SUBLANES)
    vec1 = _pad_rows(jnp.concatenate([norm_g[1:2], ple_norm_g[1:2], final_norm_g[None]], axis=0), SUBLANES)
    w_in_a_b = w_in_a[0].astype(BF16)
    w_out_a_b = w_out_a[0].astype(BF16)
    wg = jnp.concatenate([w_rgate[0], w_igate[0]], axis=-1).astype(BF16)
    wpg = w_ple_gate.astype(BF16)
    wpp = w_ple_proj.astype(BF16)
    w_in_b_b = w_in_b[0].astype(BF16)
    w_out_b_b = w_out_b[0].astype(BF16)
    sinks_p = sinks[0].reshape(1, N_HEADS)

    h1_p, hl_p, tail_p = _prompt_l0(x_prompt, p_prompt, vec0, w_in_a_b, wg, w_out_a_b, wpg[0], wpp[0])
    y_prompt, kl_p, vl_p = _prompt_l1(sinks_p, h1_p, p_prompt, vec1, w_in_b_b, w_out_b_b, wpg[1], wpp[1])
    nbp = x_prompt.shape[0]
    new_h_p = hl_p[:, SUBLANES - 1][None]
    new_conv_p = tail_p[:, SUBLANES - (CONV_W - 1):][None]
    new_k_p = kl_p.reshape(1, nbp, WINDOW, N_KV, HEAD_DIM)
    new_v_p = vl_p.reshape(1, nbp, WINDOW, N_KV, HEAD_DIM)

    rows = steps * nseq
    x_tm = jnp.swapaxes(x_sample, 0, 1).reshape(rows, D_MODEL)
    p_tm = jnp.swapaxes(p_sample, 1, 2).reshape(2, rows, D_PLE)
    cst_tm = jnp.swapaxes(state_conv[0], 0, 1)
    h1_s, hl_s, cnew_tm, q_s, k_s, v_s, gate_s = _sample_l0(
        x_tm, p_tm[0], state_lru_h[0], cst_tm, vec0, vec1, w_in_a_b, wg, w_out_a_b, wpg[0], wpp[0], w_in_b_b)

    sb = SAMPLE_SEQS
    q16 = q_s.reshape(steps, nseq, N_KV, GROUP, HEAD_DIM).transpose(1, 3, 0, 2, 4).reshape(nseq, GROUP * steps, KV_W)

    def new_rows_t(a):
        a = a.reshape(steps, nseq // sb, sb, KV_W).transpose(1, 3, 2, 0).reshape(nseq // sb, KV_W, sb * steps)
        return jnp.concatenate([a, jnp.zeros((nseq // sb, KV_W, LANES - sb * steps), a.dtype)], axis=-1)

    kc = cache_k[0].transpose(0, 2, 3, 1).reshape(nseq, KV_W, WINDOW)
    vc = cache_v[0].transpose(0, 2, 3, 1).reshape(nseq, KV_W, WINDOW)
    head = jnp.arange(N_KV * GROUP * steps) // steps
    slopes = jnp.exp2(-8.0 * (head + 1).astype(F32) / N_HEADS)
    tq = (jnp.arange(N_KV * GROUP * steps) % steps).astype(F32)
    rowinfo = jnp.stack([sinks[0][head], slopes, tq], axis=1)
    rowinfo = jnp.concatenate([rowinfo, jnp.zeros((rowinfo.shape[0], LANES - 3), F32)], axis=1)
    o16, nk, nv = _sample_attn(q16, new_rows_t(k_s), new_rows_t(v_s), kc, vc, rowinfo)

    o_tm = o16.reshape(nseq, GROUP, steps, N_KV, HEAD_DIM).transpose(2, 0, 3, 1, 4).reshape(rows, Q_W)
    y_tm = _sample_tail(o_tm, gate_s, h1_s, p_tm[1], vec1, w_out_b_b, wpg[1], wpp[1])
    y_sample = jnp.swapaxes(y_tm.reshape(steps, nseq, D_MODEL), 0, 1)
    new_conv_s = jnp.swapaxes(cnew_tm, 0, 1)[None]
    new_k_s = nk.reshape(nseq, N_KV, HEAD_DIM, WINDOW).transpose(0, 3, 1, 2)[None]
    new_v_s = nv.reshape(nseq, N_KV, HEAD_DIM, WINDOW).transpose(0, 3, 1, 2)[None]

    return (y_prompt, y_sample, new_h_p, new_conv_p, new_k_p, new_v_p,
            hl_s[None], new_conv_s, new_k_s, new_v_s)
```

```python
import functools

import jax
import jax.numpy as jnp
from jax import lax
from jax.experimental import pallas as pl
from jax.experimental.pallas import tpu as pltpu

D_MODEL = 1024
D_RNN = 1024
D_PLE = 256
N_LRU_BLOCKS = 8
LRU_BLOCK = 128
CONV_W = 4
LRU_C = 8.0
EPS = 1e-6
HEAD_DIM = 64
N_HEADS = 16
N_KV = 4
GROUP = 4
Q_W = 1024
KV_W = 256
WINDOW = 128
BLOCK = 128
NEG_INF = -1e30
QK_SCALE = HEAD_DIM ** -0.5
SQRT_FLOOR = 1e-37

LANES = 128
SUBLANES = 8
VMEM_LIMIT = 56 * 1024 * 1024

PROMPT_TILE = 256
SAMPLE_SEQS = 16
ATTN_LAG = 4
SAMPLE_LAG = 3

F32 = jnp.float32
BF16 = jnp.bfloat16


def _rms(x, g):
    ms = jnp.mean(x * x, axis=-1, keepdims=True)
    return x * lax.rsqrt(ms + EPS) * g


def _sigmoid(x):
    return 0.5 * jnp.tanh(0.5 * x) + 0.5


def _silu(x):
    return x * _sigmoid(x)


def _dot(a, w):
    return jnp.dot(a.astype(BF16), w, preferred_element_type=F32)


def _softplus(z):
    return jnp.maximum(z, 0.0) + jnp.log1p(jnp.exp(-jnp.abs(z)))


def _lru_gate_logits(xc, wg_ref):
    xcb = xc.astype(BF16)
    zr, zi = [], []
    for n in range(N_LRU_BLOCKS):
        z = jnp.dot(xcb[:, n * LRU_BLOCK:(n + 1) * LRU_BLOCK], wg_ref[n], preferred_element_type=F32)
        zr.append(z[:, :LRU_BLOCK])
        zi.append(z[:, LRU_BLOCK:])
    return jnp.concatenate(zr, axis=1), jnp.concatenate(zi, axis=1)


def _lru_coeffs(xc, logits, b_r, b_i, lam):
    r = _sigmoid(logits[0] + b_r)
    i = _sigmoid(logits[1] + b_i)
    log_a = (-LRU_C * _softplus(-lam)) * r
    a = jnp.exp(log_a)
    y = -jnp.tanh(log_a) * (a * a + 1.0)
    b = (y * lax.rsqrt(jnp.maximum(y, SQRT_FLOOR))) * (i * xc)
    return a, b


def _ple(h, p, g, wpg_ref, wpp_ref):
    gate = _sigmoid(_dot(_rms(h, g), wpg_ref[...]))
    return h + gate * _dot(p, wpp_ref[...])


CHUNK = 4
HDR = SUBLANES


def _local_scans(aq, bq):
    big_a, big_b = [aq[0]], [bq[0]]
    for q in range(1, CHUNK):
        big_a.append(aq[q] * big_a[q - 1])
        big_b.append(aq[q] * big_b[q - 1] + bq[q])
    return big_a, big_b


def _carry_in(i, a_tot, b_tot, h0, levels):
    m = a_tot.shape[0]
    if m == CHUNK:
        h, cin = h0, []
        for r in range(CHUNK):
            cin.append(h)
            h = a_tot[r:r + 1] * h + b_tot[r:r + 1]
        return jnp.concatenate(cin, axis=0), h
    a_buf, b_buf, h_buf = levels[0]
    a_buf[i] = a_tot
    b_buf[i] = b_tot
    mq = m // CHUNK
    aq = [a_buf[i, pl.ds(q, mq, stride=CHUNK), :] for q in range(CHUNK)]
    bq = [b_buf[i, pl.ds(q, mq, stride=CHUNK), :] for q in range(CHUNK)]
    big_a, big_b = _local_scans(aq, bq)
    cin, h_last = _carry_in(i, big_a[-1], big_b[-1], h0, levels[1:])
    h_buf[i, HDR - 1:HDR, :] = h0
    for q in range(CHUNK):
        h_buf[i, pl.ds(HDR + q, mq, stride=CHUNK), :] = big_b[q] + big_a[q] * cin
    return h_buf[i, HDR - 1:HDR - 1 + m, :], h_last


V0_NORM, V0_PLE, V0_CB, V0_BR, V0_BI, V0_LAM, V0_CW = 0, 1, 2, 3, 4, 5, 6
V1_NORM, V1_PLE, V1_FINAL = 0, 1, 2


def _prompt_l0_kernel(x_ref, p_ref, vec_ref, win_ref, wg_ref, wout_ref, wpg_ref, wpp_ref,
                      h1_ref, hl_ref, tail_ref,
                      hcar, xb_s, xc_s, zr_s, zi_s, h_s, l2a, l2b, l2h, l3a, l3b, l3h):
    t_rows = x_ref.shape[1]
    nslab = D_RNN // LANES
    levels = ((l2a, l2b, l2h), (l3a, l3b, l3h))

    @pl.when(pl.program_id(0) == 0)
    def _():
        hcar[...] = jnp.zeros_like(hcar)
        xb_s[:, 0:HDR, :] = jnp.zeros((xb_s.shape[0], HDR, LANES), F32)

    st = [dict() for _ in range(x_ref.shape[0])]

    def in_proj(n):
        st[n]["xg"] = _dot(_rms(x_ref[n], vec_ref[V0_NORM:V0_NORM + 1]), win_ref[...])
        st[n]["pp"] = _dot(p_ref[n], wpp_ref[...])

    def conv_gates(n):
        xg = st[n]["xg"]
        tails = []
        for l in range(nslab):
            i = n * nslab + l
            lanes = slice(l * LANES, (l + 1) * LANES)
            xb = xg[:, lanes]
            xb_s[i, HDR:HDR + t_rows, :] = xb
            acc = vec_ref[V0_CB:V0_CB + 1, lanes] + vec_ref[V0_CW + 3:V0_CW + 4, lanes] * xb
            for k in range(1, CONV_W):
                acc = acc + vec_ref[V0_CW + 3 - k:V0_CW + 4 - k, lanes] * xb_s[i, HDR - k:HDR - k + t_rows, :]
            xc_s[i] = acc
            z = jnp.dot(acc.astype(BF16), wg_ref[l], preferred_element_type=F32)
            zr_s[i] = z[:, :LRU_BLOCK]
            zi_s[i] = z[:, LRU_BLOCK:]
            tail = xb[t_rows - HDR:]
            xb_s[i, 0:HDR, :] = tail
            tails.append(tail)
        tail_ref[n] = jnp.concatenate(tails, axis=1)

    def recurrence(n):
        nchunk = t_rows // CHUNK
        for l in range(nslab):
            i = n * nslab + l
            lanes = slice(l * LANES, (l + 1) * LANES)
            aq, bq = [], []
            for q in range(CHUNK):
                rows = pl.ds(q, nchunk, stride=CHUNK)
                a, b = _lru_coeffs(xc_s[i, rows, :], (zr_s[i, rows, :], zi_s[i, rows, :]),
                                   vec_ref[V0_BR:V0_BR + 1, lanes], vec_ref[V0_BI:V0_BI + 1, lanes],
                                   vec_ref[V0_LAM:V0_LAM + 1, lanes])
                aq.append(a)
                bq.append(b)
            big_a, big_b = _local_scans(aq, bq)
            cin, h_last = _carry_in(i, big_a[-1], big_b[-1], hcar[n, :, lanes], levels)
            for q in range(CHUNK):
                h_s[i, pl.ds(q, nchunk, stride=CHUNK), :] = big_b[q] + big_a[q] * cin
            hcar[n, :, lanes] = h_last
        h = jnp.concatenate([h_s[n * nslab + l] for l in range(nslab)], axis=1)
        hl_ref[n] = h[t_rows - SUBLANES:]
        st[n]["hg"] = h * _silu(st[n]["xg"][:, D_RNN:])

    def out_proj(n):
        st[n]["h1"] = x_ref[n] + _dot(st[n]["hg"], wout_ref[...])

    def ple(n):
        h1 = st[n]["h1"]
        gl = _dot(_rms(h1, vec_ref[V0_PLE:V0_PLE + 1]), wpg_ref[...])
        h1_ref[n] = h1 + _sigmoid(gl) * st[n]["pp"]

    for stage in (in_proj, conv_gates):
        for n in range(len(st)):
            stage(n)
    for n in range(len(st)):
        recurrence(n)
        out_proj(n)
    for n in range(len(st)):
        ple(n)


def _prompt_l0_skew_kernel(x_ref, xn_ref, p_ref, vec_ref, win_ref, wg_ref, wout_ref, wpg_ref, wpp_ref,
                           h1_ref, hl_ref, tail_ref,
                           hcar, xb_s, xc_s, zr_s, zi_s, h_s, l2a, l2b, l2h, l3a, l3b, l3h, g_s, hl0_s, tail0_s):
    t_rows = x_ref.shape[1]
    nslab = D_RNN // LANES
    nxb = D_RNN // NCOL
    ncol = D_MODEL // NCOL
    levels = ((l2a, l2b, l2h), (l3a, l3b, l3h))
    s_idx = pl.program_id(0)

    def front(n, src, first_step):
        state = {}

        def norm():
            state["u"] = _rms(src(), vec_ref[V0_NORM:V0_NORM + 1]).astype(BF16)

        def chunk(c):
            r = jnp.dot(state["u"], win_ref[:, c * NCOL:(c + 1) * NCOL], preferred_element_type=F32)
            if c < nxb:
                for half in range(NCOL // LANES):
                    i = n * nslab + c * (NCOL // LANES) + half
                    xb_s[i, HDR:HDR + t_rows, :] = r[:, half * LANES:(half + 1) * LANES]
            else:
                g_s[n, :, (c - nxb) * NCOL:(c - nxb + 1) * NCOL] = r

        def conv_gates(l):
            i = n * nslab + l
            lanes = slice(l * LANES, (l + 1) * LANES)
            acc = vec_ref[V0_CB:V0_CB + 1, lanes]
            for k in range(CONV_W):
                acc = acc + vec_ref[V0_CW + 3 - k:V0_CW + 4 - k, lanes] * xb_s[i, HDR - k:HDR - k + t_rows, :]
            xc_s[i] = acc
            tail = xb_s[i, t_rows:t_rows + HDR, :]
            xb_s[i, 0:HDR, :] = tail
            if n == 0:
                tail0_s[:, lanes] = tail
            else:
                tail_ref[n, :, lanes] = tail

        def gate_logits(l):
            i = n * nslab + l
            z = jnp.dot(xc_s[i].astype(BF16), wg_ref[l], preferred_element_type=F32)
            zr_s[i] = z[:, :LRU_BLOCK]
            zi_s[i] = z[:, LRU_BLOCK:]

        def recurrence(l):
            i = n * nslab + l
            lanes = slice(l * LANES, (l + 1) * LANES)
            nchunk = t_rows // CHUNK
            aq, bq = [], []
            for q in range(CHUNK):
                rows = pl.ds(q, nchunk, stride=CHUNK)
                a, b = _lru_coeffs(xc_s[i, rows, :], (zr_s[i, rows, :], zi_s[i, rows, :]),
                                   vec_ref[V0_BR:V0_BR + 1, lanes], vec_ref[V0_BI:V0_BI + 1, lanes],
                                   vec_ref[V0_LAM:V0_LAM + 1, lanes])
                aq.append(a)
                bq.append(b)
            big_a, big_b = _local_scans(aq, bq)
            cin, h_last = _carry_in(i, big_a[-1], big_b[-1], hcar[n, :, lanes], levels)
            for q in range(CHUNK):
                h_s[i, pl.ds(q, nchunk, stride=CHUNK), :] = big_b[q] + big_a[q] * cin
            hcar[n, :, lanes] = h_last
            last_rows = h_s[i, t_rows - SUBLANES:t_rows, :]
            if n == 0:
                hl0_s[:, lanes] = last_rows
            else:
                hl_ref[n, :, lanes] = last_rows

        del first_step
        return dict(norm=norm, chunk=[functools.partial(chunk, c) for c in range(2 * nxb)],
                    conv=[functools.partial(conv_gates, l) for l in range(nslab)],
                    logits=[functools.partial(gate_logits, l) for l in range(nslab)],
                    rec=[functools.partial(recurrence, l) for l in range(nslab)])

    def back(n):
        state = {}

        def gate():
            h = jnp.concatenate([h_s[n * nslab + l] for l in range(nslab)], axis=1)
            state["hg"] = (h * _silu(g_s[n])).astype(BF16)

        def out_chunk(c):
            cols = slice(c * NCOL, (c + 1) * NCOL)
            state["h1", c] = x_ref[n, :, cols] + jnp.dot(state["hg"], wout_ref[:, cols], preferred_element_type=F32)

        def norm():
            state["h1"] = jnp.concatenate([state["h1", c] for c in range(ncol)], axis=1)
            state["nrm"] = _rms(state["h1"], vec_ref[V0_PLE:V0_PLE + 1]).astype(BF16)
            state["pp"] = _dot(p_ref[n], wpp_ref[...])

        def gate_chunk(c):
            cols = slice(c * NCOL, (c + 1) * NCOL)
            gl = jnp.dot(state["nrm"], wpg_ref[:, cols], preferred_element_type=F32)
            h1_ref[n, :, cols] = state["h1"][:, cols] + _sigmoid(gl) * state["pp"][:, cols]

        return ([gate] + [functools.partial(out_chunk, c) for c in range(ncol)] + [norm]
                + [functools.partial(gate_chunk, c) for c in range(ncol)])

    def run_pair(f, back_tasks):
        chunk, conv, logits, rec = f["chunk"], f["conv"], f["logits"], f["rec"]
        gate, outs, norm, gcs = back_tasks[0], back_tasks[1:1 + ncol], back_tasks[1 + ncol], back_tasks[2 + ncol:]
        seq = [f["norm"]] + chunk[:nxb] + [gate] + conv[:nslab // 2] + chunk[nxb:nxb + 2] + conv[nslab // 2:]
        seq += chunk[nxb + 2:] + logits[:2] + [rec[0], logits[2], rec[1], logits[3]]
        for k in range(ncol):
            seq += [outs[k], rec[2 + k], logits[4 + k]]
        seq += [norm, rec[6], gcs[0], rec[7]] + gcs[1:]
        assert len(seq) == 1 + len(chunk) + len(conv) + len(logits) + len(rec) + len(back_tasks)
        for task in seq:
            task()

    @pl.when(s_idx == 0)
    def _():
        hcar[...] = jnp.zeros_like(hcar)
        xb_s[:, 0:HDR, :] = jnp.zeros((xb_s.shape[0], HDR, LANES), F32)
        f0 = front(0, lambda: x_ref[0], True)
        for task in [f0["norm"]] + f0["chunk"] + f0["conv"] + f0["logits"] + f0["rec"]:
            task()

    hl_ref[0] = hl0_s[...]
    tail_ref[0] = tail0_s[...]
    run_pair(front(1, lambda: x_ref[1], False), back(0))
    run_pair(front(0, lambda: xn_ref[...], False), back(1))


def _const_spec(shape):
    return pl.BlockSpec(shape, lambda *_: (0,) * len(shape), pipeline_mode=pl.Buffered(1))


def _prompt_l0(x, p_all, vec0, w_in_a, wg, w_out_a, wpg0, wpp0):
    nb, seq, _ = x.shape
    t = PROMPT_TILE
    assert t % CHUNK ** 3 == 0 and t // CHUNK ** 3 == CHUNK, "three strided levels, then a CHUNK-row serial scan"
    nslab = nb * (D_RNN // LANES)
    return pl.pallas_call(
        _prompt_l0_skew_kernel,
        grid=(seq // t,),
        in_specs=[
            pl.BlockSpec((nb, t, D_MODEL), lambda s: (0, s, 0)),
            pl.BlockSpec((None, t, D_MODEL), lambda s: (0, jnp.minimum(s + 1, seq // t - 1), 0)),
            pl.BlockSpec((None, nb, t, D_PLE), lambda s: (0, 0, s, 0)),
            _const_spec(vec0.shape), _const_spec(w_in_a.shape), _const_spec(wg.shape),
            _const_spec(w_out_a.shape), _const_spec(wpg0.shape), _const_spec(wpp0.shape),
        ],
        out_specs=[
            pl.BlockSpec((nb, t, D_MODEL), lambda s: (0, s, 0)),
            pl.BlockSpec((nb, SUBLANES, D_RNN), lambda s: (0, 0, 0)),
            pl.BlockSpec((nb, SUBLANES, D_RNN), lambda s: (0, 0, 0)),
        ],
        out_shape=[
            jax.ShapeDtypeStruct((nb, seq, D_MODEL), F32),
            jax.ShapeDtypeStruct((nb, SUBLANES, D_RNN), F32),
            jax.ShapeDtypeStruct((nb, SUBLANES, D_RNN), F32),
        ],
        scratch_shapes=[pltpu.VMEM((nb, 1, D_RNN), F32), pltpu.VMEM((nslab, HDR + t, LANES), F32)]
        + [pltpu.VMEM((nslab, t, LANES), F32)] * 4
        + [pltpu.VMEM((nslab, rows, LANES), F32)
           for m in (t // CHUNK, t // CHUNK ** 2) for rows in (m, m, HDR + m)]
        + [pltpu.VMEM((nb, t, D_RNN), F32), pltpu.VMEM((SUBLANES, D_RNN), F32), pltpu.VMEM((SUBLANES, D_RNN), F32)],
        compiler_params=pltpu.CompilerParams(
            dimension_semantics=("arbitrary",), vmem_limit_bytes=VMEM_LIMIT),
        name="prompt_l0",
    )(x, x, p_all, vec0, w_in_a, wg, w_out_a, wpg0, wpp0)


def _slope(head):
    return 2.0 ** (-8.0 * (head + 1) / N_HEADS)


def _build_prompt_bias(bias_ref):
    shape = (BLOCK, 4 * BLOCK)
    qi = lax.broadcasted_iota(jnp.int32, shape, 0)
    col = lax.broadcasted_iota(jnp.int32, shape, 1)
    si = col & (2 * BLOCK - 1)
    second = col >= 2 * BLOCK
    dist = BLOCK + qi - si
    valid = (dist >= 0) & (dist < WINDOW)
    valid_first = valid & (si >= BLOCK)
    distf = dist.astype(F32)
    for pair in range(N_HEADS // 2):
        slope = jnp.where(second, _slope(2 * pair + 1), _slope(2 * pair))
        b = -(slope * distf)
        bias_ref[0, pair] = jnp.where(valid_first, b, NEG_INF)
        bias_ref[1, pair] = jnp.where(valid, b, NEG_INF)


def _both_halves(x2, kv, lo):
    if kv % 2 == 0:
        low = jnp.where(lo, x2, 0.0)
        high = pltpu.roll(low, HEAD_DIM, 1)
    else:
        high = jnp.where(lo, 0.0, x2)
        low = pltpu.roll(high, HEAD_DIM, 1)
    return jnp.concatenate([low, high], axis=0).astype(BF16)


def _pair_scores(q2, k_both, bias):
    return lax.dot_general(q2, k_both, (((1,), (1,)), ((), ())), preferred_element_type=F32) + bias


def _pair_values(s2, v_both, sinks, lo):
    es, inv = [], []
    for half in range(2):
        sh = s2[:, half * 2 * BLOCK:(half + 1) * 2 * BLOCK]
        m = jnp.maximum(jnp.max(sh, axis=-1, keepdims=True), sinks[half])
        e = jnp.exp(sh - m)
        denom = jnp.sum(e, axis=-1, keepdims=True) + jnp.exp(sinks[half] - m)
        es.append(e.astype(BF16))
        inv.append(1.0 / denom)
    o2 = jnp.dot(jnp.concatenate(es, axis=1), v_both, preferred_element_type=F32)
    return o2 * jnp.where(lo, inv[0], inv[1])


PROJ_W = Q_W + 2 * KV_W + Q_W
NCOL = 256


def _spread(n_tasks, n_slots):
    return [(s + 1) * n_tasks // n_slots - s * n_tasks // n_slots for s in range(n_slots)]


def _prompt_l1_kernel(sink_ref, h1_ref, h1n_ref, p_ref, vec_ref, win_ref, wout_ref, wpg_ref, wpp_ref,
                      y_ref, kl_ref, vl_ref, kprev, vprev, bias_ref, proj_s):
    nstreams, t_rows = h1_ref.shape[0], h1_ref.shape[1]
    assert nstreams == 2
    nblk = t_rows // BLOCK
    nchunk = PROJ_W // NCOL
    s_idx = pl.program_id(0)

    def proj_tasks(n, src):
        state = {}

        def norm():
            state["u"] = _rms(src(), vec_ref[V1_NORM:V1_NORM + 1]).astype(BF16)

        def chunk(c):
            proj_s[n, c] = jnp.dot(state["u"], win_ref[:, c * NCOL:(c + 1) * NCOL], preferred_element_type=F32)

        return [norm] + [functools.partial(chunk, c) for c in range(nchunk)]

    @pl.when(s_idx == 0)
    def _():
        _build_prompt_bias(bias_ref)
        kprev[...] = jnp.zeros_like(kprev)
        vprev[...] = jnp.zeros_like(vprev)
        for task in proj_tasks(0, lambda: h1_ref[0]):
            task()

    lo = lax.broadcasted_iota(jnp.int32, (1, LANES), 1) < HEAD_DIM
    first_tbl = jnp.where(s_idx == 0, 0, 1)
    streams = range(nstreams)

    def proj(n, rows, col0, width):
        c, off = divmod(col0, NCOL)
        assert off + width <= NCOL
        return proj_s[n, c, rows, off:off + width]

    def kv_block(n, j, base, carry_ref):
        if j < 0:
            return carry_ref[n]
        return proj(n, slice(j * BLOCK, (j + 1) * BLOCK), base, KV_W)

    cache = {}

    def masked_kv(n, j, kv, base, carry_ref):
        key = (n, j, kv, base)
        if key not in cache:
            lanes = slice((kv // 2) * LANES, (kv // 2 + 1) * LANES)
            prev, cur = kv_block(n, j - 1, base, carry_ref), kv_block(n, j, base, carry_ref)
            cache[key] = _both_halves(jnp.concatenate([prev[:, lanes], cur[:, lanes]], axis=0), kv, lo)
        return cache[key]

    def scores_of(n, j, pair):
        q2 = proj(n, slice(j * BLOCK, (j + 1) * BLOCK), pair * LANES, LANES)
        q2 = (q2 * QK_SCALE).astype(BF16)
        return _pair_scores(q2, masked_kv(n, j, pair * 2 // GROUP, Q_W, kprev),
                            bias_ref[first_tbl if j == 0 else 1, pair])

    def values_of(n, j, pair, s2):
        sinks = [sink_ref[0, 2 * pair + half] for half in range(2)]
        return _pair_values(s2, masked_kv(n, j, pair * 2 // GROUP, Q_W + KV_W, vprev), sinks, lo)

    pp = {}
    o_blocks = {n: [] for n in streams}

    def ple_proj_task(n):
        def run():
            pp[n] = _dot(p_ref[n], wpp_ref[...])
        return run

    def tail_tasks(n):
        state = {}
        ncol = D_MODEL // NCOL

        def start():
            k_last = kv_block(n, nblk - 1, Q_W, kprev)
            v_last = kv_block(n, nblk - 1, Q_W + KV_W, vprev)
            kprev[n] = k_last
            vprev[n] = v_last
            kl_ref[n] = k_last
            vl_ref[n] = v_last
            gate = jnp.concatenate([proj_s[n, c] for c in range((Q_W + 2 * KV_W) // NCOL, nchunk)], axis=1)
            state["hg"] = (jnp.concatenate(o_blocks[n], axis=0) * _silu(gate)).astype(BF16)

        def out_chunk(c):
            cols = slice(c * NCOL, (c + 1) * NCOL)
            state["h2", c] = h1_ref[n, :, cols] + jnp.dot(state["hg"], wout_ref[:, cols], preferred_element_type=F32)

        def norm():
            state["h2"] = jnp.concatenate([state["h2", c] for c in range(ncol)], axis=1)
            state["nrm"] = _rms(state["h2"], vec_ref[V1_PLE:V1_PLE + 1]).astype(BF16)

        def gate_chunk(c):
            cols = slice(c * NCOL, (c + 1) * NCOL)
            gl = jnp.dot(state["nrm"], wpg_ref[:, cols], preferred_element_type=F32)
            state["out", c] = state["h2"][:, cols] + _sigmoid(gl) * pp[n][:, cols]

        def final():
            out = jnp.concatenate([state["out", c] for c in range(ncol)], axis=1)
            y_ref[n] = _rms(out, vec_ref[V1_FINAL:V1_FINAL + 1])

        return ([start] + [functools.partial(out_chunk, c) for c in range(ncol)] + [norm]
                + [functools.partial(gate_chunk, c) for c in range(ncol)] + [final])

    per_block = N_HEADS // 2
    units = [(n, j, pair) for n in streams for j in range(nblk) for pair in range(per_block)]
    per_stream = len(units) // nstreams
    fill_first = proj_tasks(1, lambda: h1_ref[1]) + [ple_proj_task(0), ple_proj_task(1)]
    fill_second = tail_tasks(0)
    plan = ([fill_first.pop(0) for _ in range(k)] for k in _spread(len(fill_first), per_stream))
    plan = list(plan) + [[] for _ in range(ATTN_LAG)]
    plan += [[fill_second.pop(0) for _ in range(k)] for k in _spread(len(fill_second), per_stream)]
    scores, cols = {}, {}
    for t in range(len(units) + ATTN_LAG):
        if t < len(units):
            scores[t] = scores_of(*units[t])
        if t >= ATTN_LAG:
            n, j, pair = units[t - ATTN_LAG]
            cols[pair] = values_of(n, j, pair, scores.pop(t - ATTN_LAG))
            if len(cols) == per_block:
                o_blocks[n].append(jnp.concatenate([cols[c] for c in range(per_block)], axis=1))
                cols = {}
        for task in plan[t]:
            task()

    ahead = proj_tasks(0, lambda: h1n_ref[...])
    for pair in zip(ahead, tail_tasks(1)):
        for task in pair:
            task()


def _prompt_l1(sinks_p, h1, p_all, vec1, w_in_b, w_out_b, wpg1, wpp1):
    nb, seq, _ = h1.shape
    t = PROMPT_TILE
    return pl.pallas_call(
        _prompt_l1_kernel,
        grid=(seq // t,),
        in_specs=[
            pl.BlockSpec(memory_space=pltpu.SMEM),
            pl.BlockSpec((nb, t, D_MODEL), lambda s: (0, s, 0)),
            pl.BlockSpec((None, t, D_MODEL), lambda s: (0, jnp.minimum(s + 1, seq // t - 1), 0)),
            pl.BlockSpec((None, nb, t, D_PLE), lambda s: (1, 0, s, 0)),
            _const_spec(vec1.shape), _const_spec(w_in_b.shape), _const_spec(w_out_b.shape),
            _const_spec(wpg1.shape), _const_spec(wpp1.shape),
        ],
        out_specs=[
            pl.BlockSpec((nb, t, D_MODEL), lambda s: (0, s, 0)),
            pl.BlockSpec((nb, BLOCK, KV_W), lambda s: (0, 0, 0)),
            pl.BlockSpec((nb, BLOCK, KV_W), lambda s: (0, 0, 0)),
        ],
        out_shape=[
            jax.ShapeDtypeStruct((nb, seq, D_MODEL), F32),
            jax.ShapeDtypeStruct((nb, BLOCK, KV_W), F32),
            jax.ShapeDtypeStruct((nb, BLOCK, KV_W), F32),
        ],
        scratch_shapes=[
            pltpu.VMEM((nb, BLOCK, KV_W), F32), pltpu.VMEM((nb, BLOCK, KV_W), F32),
            pltpu.VMEM((2, 2 * GROUP, BLOCK, 4 * BLOCK), F32),
            pltpu.VMEM((nb, PROJ_W // NCOL, t, NCOL), F32),
        ],
        compiler_params=pltpu.CompilerParams(
            dimension_semantics=("arbitrary",), vmem_limit_bytes=VMEM_LIMIT),
        name="prompt_l1",
    )(sinks_p, h1, h1, p_all, vec1, w_in_b, w_out_b, wpg1, wpp1)


def _sample_l0_kernel(x_ref, p_ref, h0_ref, cst_ref, vec_ref, vec1_ref, win_ref, wg_ref, wout_ref, wpg_ref, wpp_ref,
                      winb_ref, h1_ref, hl_ref, cnew_ref, q_ref, k_ref, v_ref, gate_ref):
    nseq = h0_ref.shape[0]
    steps = x_ref.shape[0] // nseq
    x = x_ref[...]
    xg = _dot(_rms(x, vec_ref[V0_NORM:V0_NORM + 1]), win_ref[...])
    xb = xg[:, :D_RNN]
    gate = xg[:, D_RNN:]
    hist = [cst_ref[k] for k in range(CONV_W - 1)] + [xb[t * nseq:(t + 1) * nseq] for t in range(steps)]
    xc = []
    for t in range(steps):
        acc = vec_ref[V0_CB:V0_CB + 1] + vec_ref[V0_CW:V0_CW + 1] * hist[t]
        for k in range(1, CONV_W):
            acc = acc + vec_ref[V0_CW + k:V0_CW + k + 1] * hist[t + k]
        xc.append(acc)
    for k in range(CONV_W - 1):
        cnew_ref[k] = hist[steps + k]
    xc = jnp.concatenate(xc, axis=0)
    a, b = _lru_coeffs(xc, _lru_gate_logits(xc, wg_ref), vec_ref[V0_BR:V0_BR + 1], vec_ref[V0_BI:V0_BI + 1],
                       vec_ref[V0_LAM:V0_LAM + 1])
    h = h0_ref[...]
    hs = []
    for t in range(steps):
        h = a[t * nseq:(t + 1) * nseq] * h + b[t * nseq:(t + 1) * nseq]
        hs.append(h)
    hl_ref[...] = h
    h1 = x + _dot(jnp.concatenate(hs, axis=0) * _silu(gate), wout_ref[...])
    h1 = _ple(h1, p_ref[...], vec_ref[V0_PLE:V0_PLE + 1], wpg_ref, wpp_ref)
    h1_ref[...] = h1
    proj = _dot(_rms(h1, vec1_ref[V1_NORM:V1_NORM + 1]), winb_ref[...])
    q_ref[...] = proj[:, :Q_W] * QK_SCALE
    k_ref[...] = proj[:, Q_W:Q_W + KV_W]
    v_ref[...] = proj[:, Q_W + KV_W:Q_W + 2 * KV_W]
    gate_ref[...] = proj[:, Q_W + 2 * KV_W:]


def _sample_l0(x_tm, p0_tm, h0, cst_tm, vec0, vec1, w_in_a, wg, w_out_a, wpg0, wpp0, w_in_b):
    rows = x_tm.shape[0]
    nseq = h0.shape[0]
    args = (x_tm, p0_tm, h0, cst_tm, vec0, vec1, w_in_a, wg, w_out_a, wpg0, wpp0, w_in_b)
    out_shapes = [(rows, D_MODEL), (nseq, D_RNN), (CONV_W - 1, nseq, D_RNN),
                  (rows, Q_W), (rows, KV_W), (rows, KV_W), (rows, Q_W)]
    return pl.pallas_call(
        _sample_l0_kernel,
        grid=(1,),
        in_specs=[_const_spec(a.shape) for a in args],
        out_specs=[pl.BlockSpec(s, lambda *_, n=len(s): (0,) * n) for s in out_shapes],
        out_shape=[jax.ShapeDtypeStruct(s, F32) for s in out_shapes],
        compiler_params=pltpu.CompilerParams(dimension_semantics=("arbitrary",), vmem_limit_bytes=VMEM_LIMIT),
        name="sample_l0",
    )(*args)


def _sample_attn_kernel(q_ref, knt_ref, vnt_ref, kc_ref, vc_ref, rowinfo_ref, o_ref, nk_ref, nv_ref):
    nseq = q_ref.shape[0]
    nq = q_ref.shape[1]
    steps = nq // GROUP
    rows = N_KV * nq
    sink = rowinfo_ref[:, 0:1]
    slope = rowinfo_ref[:, 1:2]
    tq = rowinfo_ref[:, 2:3]
    col = lax.broadcasted_iota(jnp.int32, (rows, WINDOW), 1).astype(F32)
    dist = WINDOW + tq - col
    bias_cache = jnp.where(dist < WINDOW, -(slope * dist), NEG_INF)
    lane_grp = lax.broadcasted_iota(jnp.int32, (1, KV_W), 1) // HEAD_DIM
    keep = lax.broadcasted_iota(jnp.int32, (1, WINDOW), 1) < WINDOW - steps
    knt_all = knt_ref[...]
    vnt_all = vnt_ref[...]
    knt_bf = knt_all.astype(BF16)
    vnt_bf = vnt_all.astype(BF16)

    def scores_of(bl):
        q = q_ref[bl]
        q_full = jnp.concatenate([jnp.where(lane_grp == g, q, 0.0) for g in range(N_KV)], axis=0).astype(BF16)
        kt_ext = jnp.concatenate([kc_ref[bl].astype(BF16), knt_bf], axis=1)
        j = col - float(steps * bl)
        dist_new = tq - j
        bias_new = jnp.where((j >= 0.0) & (dist_new >= 0.0), -(slope * dist_new), NEG_INF)
        return jnp.dot(q_full, kt_ext, preferred_element_type=F32) + jnp.concatenate([bias_cache, bias_new], axis=1)

    def exp_of(s):
        m = jnp.maximum(jnp.max(s, axis=-1, keepdims=True), sink)
        return jnp.exp(s - m), jnp.exp(sink - m)

    def values_of(bl, e, e_sink):
        denom = jnp.sum(e, axis=-1, keepdims=True) + e_sink
        vt_ext = jnp.concatenate([vc_ref[bl].astype(BF16), vnt_bf], axis=1)
        o = lax.dot_general(e.astype(BF16), vt_ext, (((1,), (1,)), ((), ())),
                            preferred_element_type=F32) * (1.0 / denom)
        acc = jnp.where(lane_grp == 0, o[:nq], 0.0)
        for g in range(1, N_KV):
            acc = acc + jnp.where(lane_grp == g, o[g * nq:(g + 1) * nq], 0.0)
        o_ref[bl] = acc

    def split3(x):
        hi = x.astype(BF16)
        r1 = x - hi.astype(F32)
        mid = r1.astype(BF16)
        lo = (r1 - mid.astype(F32)).astype(BF16)
        return jnp.concatenate([hi, mid, lo], axis=1)

    knt3 = split3(knt_all)
    vnt3 = split3(vnt_all)
    src_lane = lax.broadcasted_iota(jnp.int32, (3 * LANES, WINDOW), 0) % LANES
    dst_lane = lax.broadcasted_iota(jnp.int32, (3 * LANES, WINDOW), 1)

    def shifted(cache_ref, new3, bl):
        old = pltpu.roll(cache_ref[bl], WINDOW - steps, 1)
        pick = (src_lane == dst_lane - (WINDOW - steps) + steps * bl) & (dst_lane >= WINDOW - steps)
        new = jnp.dot(new3, jnp.where(pick, 1.0, 0.0).astype(BF16), preferred_element_type=F32)
        return jnp.where(keep, old, new)

    scores = {}
    for t in range(nseq + SAMPLE_LAG):
        if t < nseq:
            scores[t] = scores_of(t)
        if t >= SAMPLE_LAG:
            bl = t - SAMPLE_LAG
            e, e_sink = exp_of(scores.pop(bl))
            nk_ref[bl] = shifted(kc_ref, knt3, bl)
            values_of(bl, e, e_sink)
            nv_ref[bl] = shifted(vc_ref, vnt3, bl)


def _sample_attn(q16, knt, vnt, kc, vc, rowinfo):
    nseq_total, nq, _ = q16.shape
    sb = SAMPLE_SEQS
    blk3 = lambda i: (i, 0, 0)
    return pl.pallas_call(
        _sample_attn_kernel,
        grid=(nseq_total // sb,),
        in_specs=[
            pl.BlockSpec((sb, nq, KV_W), blk3),
            pl.BlockSpec((None, KV_W, LANES), blk3),
            pl.BlockSpec((None, KV_W, LANES), blk3),
            pl.BlockSpec((sb, KV_W, WINDOW), blk3),
            pl.BlockSpec((sb, KV_W, WINDOW), blk3),
            pl.BlockSpec(rowinfo.shape, lambda i: (0, 0)),
        ],
        out_specs=[
            pl.BlockSpec((sb, nq, KV_W), blk3),
            pl.BlockSpec((sb, KV_W, WINDOW), blk3),
            pl.BlockSpec((sb, KV_W, WINDOW), blk3),
        ],
        out_shape=[
            jax.ShapeDtypeStruct((nseq_total, nq, KV_W), F32),
            jax.ShapeDtypeStruct((nseq_total, KV_W, WINDOW), F32),
            jax.ShapeDtypeStruct((nseq_total, KV_W, WINDOW), F32),
        ],
        compiler_params=pltpu.CompilerParams(dimension_semantics=("arbitrary",), vmem_limit_bytes=VMEM_LIMIT),
        name="sample_attn",
    )(q16, knt, vnt, kc, vc, rowinfo)


def _sample_tail_kernel(o_ref, gate_ref, h1_ref, p_ref, vec_ref, wout_ref, wpg_ref, wpp_ref, y_ref):
    h1 = h1_ref[...]
    h2 = h1 + _dot(o_ref[...] * _silu(gate_ref[...]), wout_ref[...])
    h2 = _ple(h2, p_ref[...], vec_ref[V1_PLE:V1_PLE + 1], wpg_ref, wpp_ref)
    y_ref[...] = _rms(h2, vec_ref[V1_FINAL:V1_FINAL + 1])


def _sample_tail(o_tm, gate, h1, p1_tm, vec1, w_out_b, wpg1, wpp1):
    args = (o_tm, gate, h1, p1_tm, vec1, w_out_b, wpg1, wpp1)
    return pl.pallas_call(
        _sample_tail_kernel,
        grid=(1,),
        in_specs=[_const_spec(a.shape) for a in args],
        out_specs=pl.BlockSpec(h1.shape, lambda i: (0, 0)),
        out_shape=jax.ShapeDtypeStruct(h1.shape, F32),
        compiler_params=pltpu.CompilerParams(dimension_semantics=("arbitrary",), vmem_limit_bytes=VMEM_LIMIT),
        name="sample_tail",
    )(*args)


def _pad_rows(v, rows):
    return jnp.concatenate([v, jnp.zeros((rows - v.shape[0], v.shape[1]), v.dtype)], axis=0)


def kernel(x_prompt, x_sample, p_prompt, p_sample, state_lru_h, state_conv, cache_k, cache_v, norm_g, final_norm_g, ple_norm_g, w_ple_gate, w_ple_proj, w_in_a, conv_w_a, conv_b_a, w_rgate, b_rgate, w_igate, b_igate, lru_lambda, w_out_a, w_in_b, sinks, w_out_b):
    assert norm_g.shape[0] == 2 and w_in_a.shape[0] == 1 and w_in_b.shape[0] == 1
    nseq, steps, _ = x_sample.shape

    vec0 = _pad_rows(jnp.concatenate([norm_g[0:1], ple_norm_g[0:1], conv_b_a, b_rgate, b_igate, lru_lambda,
                                      conv_w_a[0]], axis=0), 2 * SUBLANES)
    vec1 = _pad_rows(jnp.concatenate([norm_g[1:2], ple_norm_g[1:2], final_norm_g[None]], axis=0), SUBLANES)
    w_in_a_b = w_in_a[0].astype(BF16)
    w_out_a_b = w_out_a[0].astype(BF16)
    wg = jnp.concatenate([w_rgate[0], w_igate[0]], axis=-1).astype(BF16)
    wpg = w_ple_gate.astype(BF16)
    wpp = w_ple_proj.astype(BF16)
    w_in_b_b = w_in_b[0].astype(BF16)
    w_out_b_b = w_out_b[0].astype(BF16)
    sinks_p = sinks[0].reshape(1, N_HEADS)

    h1_p, hl_p, tail_p = _prompt_l0(x_prompt, p_prompt, vec0, w_in_a_b, wg, w_out_a_b, wpg[0], wpp[0])
    y_prompt, kl_p, vl_p = _prompt_l1(sinks_p, h1_p, p_prompt, vec1, w_in_b_b, w_out_b_b, wpg[1], wpp[1])
    nbp = x_prompt.shape[0]
    new_h_p = hl_p[:, SUBLANES - 1][None]
    new_conv_p = tail_p[:, SUBLANES - (CONV_W - 1):][None]
    new_k_p = kl_p.reshape(1, nbp, WINDOW, N_KV, HEAD_DIM)
    new_v_p = vl_p.reshape(1, nbp, WINDOW, N_KV, HEAD_DIM)

    rows = steps * nseq
    x_tm = jnp.swapaxes(x_sample, 0, 1).reshape(rows, D_MODEL)
    p_tm = jnp.swapaxes(p_sample, 1, 2).reshape(2, rows, D_PLE)
    cst_tm = jnp.swapaxes(state_conv[0], 0, 1)
    h1_s, hl_s, cnew_tm, q_s, k_s, v_s, gate_s = _sample_l0(
        x_tm, p_tm[0], state_lru_h[0], cst_tm, vec0, vec1, w_in_a_b, wg, w_out_a_b, wpg[0], wpp[0], w_in_b_b)

    sb = SAMPLE_SEQS
    q16 = q_s.reshape(steps, nseq, N_KV, GROUP, HEAD_DIM).transpose(1, 3, 0, 2, 4).reshape(nseq, GROUP * steps, KV_W)

    def new_rows_t(a):
        a = a.reshape(steps, nseq // sb, sb, KV_W).transpose(1, 3, 2, 0).reshape(nseq // sb, KV_W, sb * steps)
        return jnp.concatenate([a, jnp.zeros((nseq // sb, KV_W, LANES - sb * steps), a.dtype)], axis=-1)

    kc = cache_k[0].transpose(0, 2, 3, 1).reshape(nseq, KV_W, WINDOW)
    vc = cache_v[0].transpose(0, 2, 3, 1).reshape(nseq, KV_W, WINDOW)
    head = jnp.arange(N_KV * GROUP * steps) // steps
    slopes = jnp.exp2(-8.0 * (head + 1).astype(F32) / N_HEADS)
    tq = (jnp.arange(N_KV * GROUP * steps) % steps).astype(F32)
    rowinfo = jnp.stack([sinks[0][head], slopes, tq], axis=1)
    rowinfo = jnp.concatenate([rowinfo, jnp.zeros((rowinfo.shape[0], LANES - 3), F32)], axis=1)
    o16, nk, nv = _sample_attn(q16, new_rows_t(k_s), new_rows_t(v_s), kc, vc, rowinfo)

    o_tm = o16.reshape(nseq, GROUP, steps, N_KV, HEAD_DIM).transpose(2, 0, 3, 1, 4).reshape(rows, Q_W)
    y_tm = _sample_tail(o_tm, gate_s, h1_s, p_tm[1], vec1, w_out_b_b, wpg[1], wpp[1])
    y_sample = jnp.swapaxes(y_tm.reshape(steps, nseq, D_MODEL), 0, 1)
    new_conv_s = jnp.swapaxes(cnew_tm, 0, 1)[None]
    new_k_s = nk.reshape(nseq, N_KV, HEAD_DIM, WINDOW).transpose(0, 3, 1, 2)[None]
    new_v_s = nv.reshape(nseq, N_KV, HEAD_DIM, WINDOW).transpose(0, 3, 1, 2)[None]

    return (y_prompt, y_sample, new_h_p, new_conv_p, new_k_p, new_v_p,
            hl_s[None], new_conv_s, new_k_s, new_v_s)
```
